```python
import functools
import jax
import jax.numpy as jnp
from jax import lax
import numpy as np

D_MODEL = 2048
BATCH = 1
SEQ = 16384
DEPTH = 1
DEC_BATCH = 32
DEC_SEQ = 8
PAST_LEN = 16384
PAGE_SIZE = 128

N_HEADS = 8
HEAD_DIM = 128
ATTN_WIDTH = N_HEADS * HEAD_DIM
MOBA_BLOCK = 256
MOBA_TOPK = 3
MOBA_QUERY_BLOCK = 64
ROPE_THETA = 10000.0
SSM_INNER = D_MODEL
SSM_HEAD_DIM = 64
SSM_HEADS = SSM_INNER // SSM_HEAD_DIM
SSM_GROUPS = 4
SSM_STATE = 128
SSM_CONV = 4
SSM_CHUNK = 128
CONV_DIM = SSM_INNER + 2 * SSM_GROUPS * SSM_STATE
D_FF = ((8 * D_MODEL // 3 + 255) // 256) * 256
FFN_CONV = 3
DN_ALPHA = (2.0 * DEPTH) ** 0.25
DN_BETA = (8.0 * DEPTH) ** -0.25
LN_EPS = 1e-5
RMS_EPS = 1e-5
IN_WIDTHS = (ATTN_WIDTH, ATTN_WIDTH, ATTN_WIDTH, SSM_INNER, CONV_DIM, SSM_HEADS, D_MODEL, D_MODEL)
IN_SPLITS = tuple(sum(IN_WIDTHS[:i + 1]) for i in range(len(IN_WIDTHS) - 1))
IN_DIM = sum(IN_WIDTHS)

kernel_name = 'moba_mamba2_hybrid_step'


def layer_norm(x, g, b):
    xf = x.astype(jnp.float32)
    mu = jnp.mean(xf, axis=-1, keepdims=True)
    var = jnp.mean(jnp.square(xf - mu), axis=-1, keepdims=True)
    y = (xf - mu) * lax.rsqrt(var + LN_EPS) * g.astype(jnp.float32) + b.astype(jnp.float32)
    return y.astype(x.dtype)


def group_rms_norm(y, g):
    yg = y.reshape(y.shape[:-1] + (SSM_GROUPS, SSM_INNER // SSM_GROUPS))
    yg = yg * lax.rsqrt(jnp.mean(jnp.square(yg), axis=-1, keepdims=True) + RMS_EPS)
    return yg.reshape(y.shape) * g.astype(jnp.float32)


def rope(x, pos):
    half = HEAD_DIM // 2
    inv = ROPE_THETA ** (-jnp.arange(half, dtype=jnp.float32) * 2.0 / HEAD_DIM)
    ang = pos.astype(jnp.float32)[:, None] * inv[None, :]
    cos = jnp.cos(ang)[None, :, None, :]
    sin = jnp.sin(ang)[None, :, None, :]
    xf = x.astype(jnp.float32)
    x1, x2 = xf[..., :half], xf[..., half:]
    return jnp.concatenate([x1 * cos - x2 * sin, x2 * cos + x1 * sin], axis=-1).astype(x.dtype)


def causal_dwconv(x_ext, w, b):
    width = w.shape[0]
    L = x_ext.shape[1] - width + 1
    acc = x_ext[:, 0:L] * w[0]
    for kk in range(1, width):
        acc = acc + x_ext[:, kk:kk + L] * w[kk]
    return acc + b


def moba_attend(q, qpos, kb, vb, kmean):
    nq = q.shape[0]
    nblk = kb.shape[1]
    n_sel = min(MOBA_TOPK, nblk)
    own = qpos // MOBA_BLOCK
    gate = jnp.einsum('qhd,hnd->qhn', q.astype(jnp.float32), kmean)
    fully_past = jnp.arange(nblk)[None, None, :] < own[:, None, None]
    gate = jnp.where(fully_past, gate, -jnp.inf)
    _, sel = lax.top_k(gate, n_sel)
    idx = jnp.concatenate([sel, jnp.broadcast_to(own[:, None, None], (nq, N_HEADS, 1)).astype(sel.dtype)], axis=-1)
    head_ix = jnp.arange(N_HEADS)[None, :, None]
    kg = kb[head_ix, idx]
    vg = vb[head_ix, idx]
    s = jnp.einsum('qhd,qhjkd->qhjk', q, kg, preferred_element_type=jnp.float32) * (HEAD_DIM ** -0.5)
    sel_mask = (jnp.arange(n_sel)[None, :] < own[:, None])[:, None, :, None]
    own_kpos = own[:, None] * MOBA_BLOCK + jnp.arange(MOBA_BLOCK)[None, :]
    own_mask = (own_kpos <= qpos[:, None])[:, None, None, :]
    mask = jnp.concatenate([
        jnp.broadcast_to(sel_mask, (nq, N_HEADS, n_sel, MOBA_BLOCK)),
        jnp.broadcast_to(own_mask, (nq, N_HEADS, 1, MOBA_BLOCK))], axis=2)
    s = jnp.where(mask, s, -jnp.inf).reshape(nq, N_HEADS, -1)
    p = jax.nn.softmax(s, axis=-1)
    o = jnp.einsum('qhk,qhkd->qhd', p.astype(vg.dtype), vg.reshape(nq, N_HEADS, -1, HEAD_DIM),
                   preferred_element_type=jnp.float32)
    return o.astype(q.dtype)


def moba_sequence(q, qpos, k, v, q_block):
    T = k.shape[0]
    pad = (-T) % MOBA_BLOCK
    kb = jnp.pad(k, ((0, pad), (0, 0), (0, 0))).reshape(-1, MOBA_BLOCK, N_HEADS, HEAD_DIM).transpose(2, 0, 1, 3)
    vb = jnp.pad(v, ((0, pad), (0, 0), (0, 0))).reshape(-1, MOBA_BLOCK, N_HEADS, HEAD_DIM).transpose(2, 0, 1, 3)
    kmean = jnp.mean(kb.astype(jnp.float32), axis=2)
    nq = q.shape[0]
    qc = q.reshape(nq // q_block, q_block, N_HEADS, HEAD_DIM)
    pc = qpos.reshape(nq // q_block, q_block)
    o = lax.map(lambda a: moba_attend(a[0], a[1], kb, vb, kmean), (qc, pc))
    return o.reshape(nq, N_HEADS, HEAD_DIM)


def attend_prompt(q, k, v, pos):
    L = q.shape[1]
    qb = min(MOBA_QUERY_BLOCK, L)
    if L % qb:
        qb = L
    return lax.map(lambda a: moba_sequence(a[0], pos, a[1], a[2], qb), (q, k, v))


def attend_paged(cache_k_l, cache_v_l, page_table, q, k, v, pos):
    def one(a):
        qs, ks, vs, pt = a
        kp = cache_k_l[pt].reshape(-1, N_HEADS, HEAD_DIM).astype(ks.dtype)
        vp = cache_v_l[pt].reshape(-1, N_HEADS, HEAD_DIM).astype(vs.dtype)
        return moba_sequence(qs, pos, jnp.concatenate([kp, ks], axis=0),
                             jnp.concatenate([vp, vs], axis=0), qs.shape[0])
    return lax.map(one, (q, k, v, page_table))


def ssd_scan(x, dt, A, Bm, Cm, h0):
    f32 = jnp.float32
    Bsz, L, H, P = x.shape
    G, N = Bm.shape[2], Bm.shape[3]
    R = H // G
    Q = SSM_CHUNK if L >= SSM_CHUNK else L
    pad = (-L) % Q
    x = x.astype(f32)
    dt = dt.astype(f32)
    Bm = Bm.astype(f32)
    Cm = Cm.astype(f32)
    if pad:
        pw = ((0, 0), (0, pad), (0, 0), (0, 0))
        x = jnp.pad(x, pw)
        Bm = jnp.pad(Bm, pw)
        Cm = jnp.pad(Cm, pw)
        dt = jnp.pad(dt, ((0, 0), (0, pad), (0, 0)))
    nc = (L + pad) // Q
    X = (x * dt[..., None]).reshape(Bsz, nc, Q, G, R, P)
    dA = (dt * A.astype(f32)).reshape(Bsz, nc, Q, G, R).transpose(0, 3, 4, 1, 2)
    Bc = Bm.reshape(Bsz, nc, Q, G, N)
    Cc = Cm.reshape(Bsz, nc, Q, G, N)
    cs = jnp.cumsum(dA, axis=-1)
    causal = jnp.tril(jnp.ones((Q, Q), dtype=bool))
    Lmat = jnp.exp(jnp.where(causal, cs[..., :, None] - cs[..., None, :], -jnp.inf))
    CB = jnp.einsum('bclgn,bcsgn->bcgls', Cc, Bc)
    y_diag = jnp.einsum('bcgls,bgrcls,bcsgrp->bclgrp', CB, Lmat, X)
    decay_to_end = jnp.exp(cs[..., -1:] - cs)
    chunk_states = jnp.einsum('bclgn,bgrcl,bclgrp->bcgrpn', Bc, decay_to_end, X)
    chunk_decay = jnp.exp(cs[..., -1])

    def step(h, inp):
        s_c, d_c = inp
        return h * d_c[..., None, None] + s_c, h

    h_last, h_in = lax.scan(step, h0.astype(f32).reshape(Bsz, G, R, P, N),
                            (chunk_states.transpose(1, 0, 2, 3, 4, 5), chunk_decay.transpose(3, 0, 1, 2)))
    y_off = jnp.einsum('bclgn,cbgrpn,bgrcl->bclgrp', Cc, h_in, jnp.exp(cs))
    y = (y_diag + y_off).reshape(Bsz, nc * Q, H, P)[:, :L]
    return y, h_last.reshape(Bsz, H, P, N)


def hybrid_layer(x, pos, attend, conv_buf, ssm_h0, ffn_buf,
                 w_in, ssm_conv_w, ssm_conv_b, dt_bias, a_log, d_skip, ssm_norm_g,
                 w_proj_attn, w_proj_ssm, w_out, ln1_g, ln1_b,
                 w_gate, w_up, ffn_conv_w, ffn_conv_b, w_down, ln2_g, ln2_b):
    Bsz, L, _ = x.shape
    proj = x @ w_in
    q, k, v, z, xbc, dt_raw, g_attn, g_ssm = jnp.split(proj, IN_SPLITS, axis=-1)
    q = rope(q.reshape(Bsz, L, N_HEADS, HEAD_DIM), pos)
    k = rope(k.reshape(Bsz, L, N_HEADS, HEAD_DIM), pos)
    v = v.reshape(Bsz, L, N_HEADS, HEAD_DIM)
    o_attn = attend(q, k, v, pos).reshape(Bsz, L, ATTN_WIDTH)
    xbc_ext = jnp.concatenate([conv_buf.astype(xbc.dtype), xbc], axis=1)
    new_conv = xbc_ext[:, -(SSM_CONV - 1):]
    xbc_c = jax.nn.silu(causal_dwconv(xbc_ext, ssm_conv_w, ssm_conv_b))
    xs, bm, cm = jnp.split(xbc_c, (SSM_INNER, SSM_INNER + SSM_GROUPS * SSM_STATE), axis=-1)
    xs = xs.reshape(Bsz, L, SSM_HEADS, SSM_HEAD_DIM)
    dt = jax.nn.softplus(dt_raw.astype(jnp.float32) + dt_bias.astype(jnp.float32))
    A = -jnp.exp(a_log.astype(jnp.float32))
    y_s, h_new = ssd_scan(xs, dt, A, bm.reshape(Bsz, L, SSM_GROUPS, SSM_STATE),
                          cm.reshape(Bsz, L, SSM_GROUPS, SSM_STATE), ssm_h0)
    y_s = y_s + d_skip.astype(jnp.float32)[:, None] * xs.astype(jnp.float32)
    y_s = y_s.reshape(Bsz, L, SSM_INNER) * jax.nn.silu(z.astype(jnp.float32))
    y_s = group_rms_norm(y_s, ssm_norm_g).astype(x.dtype)
    merged = jax.nn.sigmoid(g_attn) * (o_attn @ w_proj_attn) + jax.nn.sigmoid(g_ssm) * (y_s @ w_proj_ssm)
    h1 = layer_norm(DN_ALPHA * x + merged @ w_out, ln1_g, ln1_b)
    g_up = h1 @ w_gate
    u = h1 @ w_up
    g_ext = jnp.concatenate([ffn_buf.astype(g_up.dtype), g_up], axis=1)
    new_ffn = g_ext[:, -(FFN_CONV - 1):]
    g_c = causal_dwconv(g_ext, ffn_conv_w, ffn_conv_b)
    f = (jax.nn.gelu(g_c, approximate=False) * u) @ w_down
    y = layer_norm(DN_ALPHA * h1 + f, ln2_g, ln2_b)
    return y, k, v, h_new, new_conv, new_ffn


def setup_inputs(seed: int = 0) -> dict:
    key = jax.random.key(seed)
    ks = jax.random.split(key, 32)
    f32 = jnp.float32
    n_pages = PAST_LEN // PAGE_SIZE
    used = DEC_BATCH * n_pages
    n_pool = used + (used + 3) // 4
    nrm = lambda k, shp, s: jax.random.normal(k, shp, f32) * s
    dt0 = jnp.exp(jax.random.uniform(ks[14], (DEPTH, SSM_HEADS), f32) * (jnp.log(0.1) - jnp.log(0.001)) + jnp.log(0.001))
    perm = jax.random.permutation(ks[7], n_pool)[:used]
    return {
        'x_prompt': nrm(ks[0], (BATCH, SEQ, D_MODEL), 1.0),
        'x_sample': nrm(ks[1], (DEC_BATCH, DEC_SEQ, D_MODEL), 1.0),
        'cache_k': nrm(ks[2], (DEPTH, n_pool, PAGE_SIZE, N_HEADS, HEAD_DIM), 1.0),
        'cache_v': nrm(ks[3], (DEPTH, n_pool, PAGE_SIZE, N_HEADS, HEAD_DIM), 1.0),
        'state_ssm': nrm(ks[4], (DEPTH, DEC_BATCH, SSM_HEADS, SSM_HEAD_DIM, SSM_STATE), 0.1),
        'state_conv': nrm(ks[5], (DEPTH, DEC_BATCH, SSM_CONV - 1, CONV_DIM), 1.0),
        'state_ffn_conv': nrm(ks[6], (DEPTH, DEC_BATCH, FFN_CONV - 1, D_FF), 1.0),
        'page_table': perm.reshape(DEC_BATCH, n_pages).astype(jnp.int32),
        'w_in': nrm(ks[8], (DEPTH, D_MODEL, IN_DIM), D_MODEL ** -0.5),
        'ssm_conv_w': nrm(ks[9], (DEPTH, SSM_CONV, CONV_DIM), SSM_CONV ** -0.5),
        'ssm_conv_b': nrm(ks[10], (DEPTH, CONV_DIM), 0.02),
        'dt_bias': dt0 + jnp.log(-jnp.expm1(-dt0)),
        'a_log': jnp.log(jax.random.uniform(ks[11], (DEPTH, SSM_HEADS), f32, 1.0, 16.0)),
        'd_skip': 1.0 + nrm(ks[12], (DEPTH, SSM_HEADS), 0.02),
        'ssm_norm_g': 1.0 + nrm(ks[13], (DEPTH, SSM_INNER), 0.02),
        'w_proj_attn': nrm(ks[15], (DEPTH, ATTN_WIDTH, D_MODEL), ATTN_WIDTH ** -0.5),
        'w_proj_ssm': nrm(ks[16], (DEPTH, SSM_INNER, D_MODEL), SSM_INNER ** -0.5),
        'w_out': nrm(ks[17], (DEPTH, D_MODEL, D_MODEL), DN_BETA * D_MODEL ** -0.5),
        'ln1_g': 1.0 + nrm(ks[18], (DEPTH, D_MODEL), 0.02),
        'ln1_b': nrm(ks[19], (DEPTH, D_MODEL), 0.02),
        'w_gate': nrm(ks[20], (DEPTH, D_MODEL, D_FF), D_MODEL ** -0.5),
        'w_up': nrm(ks[21], (DEPTH, D_MODEL, D_FF), D_MODEL ** -0.5),
        'ffn_conv_w': nrm(ks[22], (DEPTH, FFN_CONV, D_FF), FFN_CONV ** -0.5),
        'ffn_conv_b': nrm(ks[23], (DEPTH, D_FF), 0.02),
        'w_down': nrm(ks[24], (DEPTH, D_FF, D_MODEL), DN_BETA * D_FF ** -0.5),
        'ln2_g': 1.0 + nrm(ks[25], (DEPTH, D_MODEL), 0.02),
        'ln2_b': nrm(ks[26], (DEPTH, D_MODEL), 0.02),
    }


def reference(x_prompt, x_sample, cache_k, cache_v, state_ssm, state_conv, state_ffn_conv, page_table,
              w_in, ssm_conv_w, ssm_conv_b, dt_bias, a_log, d_skip, ssm_norm_g,
              w_proj_attn, w_proj_ssm, w_out, ln1_g, ln1_b,
              w_gate, w_up, ffn_conv_w, ffn_conv_b, w_down, ln2_g, ln2_b):
    bp, lp = x_prompt.shape[0], x_prompt.shape[1]
    past_len = page_table.shape[1] * PAGE_SIZE
    pos_p = jnp.arange(lp, dtype=jnp.int32)
    pos_s = past_len + jnp.arange(x_sample.shape[1], dtype=jnp.int32)
    hp, hs = x_prompt, x_sample
    kp_l, vp_l, sp_l, cp_l, fp_l = [], [], [], [], []
    ks_l, vs_l, ss_l, cs_l, fs_l = [], [], [], [], []
    for l in range(DEPTH):
        lw = (w_in[l], ssm_conv_w[l], ssm_conv_b[l], dt_bias[l], a_log[l], d_skip[l], ssm_norm_g[l],
              w_proj_attn[l], w_proj_ssm[l], w_out[l], ln1_g[l], ln1_b[l],
              w_gate[l], w_up[l], ffn_conv_w[l], ffn_conv_b[l], w_down[l], ln2_g[l], ln2_b[l])
        conv0 = jnp.zeros((bp, SSM_CONV - 1, CONV_DIM), hp.dtype)
        ssm0 = jnp.zeros((bp, SSM_HEADS, SSM_HEAD_DIM, SSM_STATE), jnp.float32)
        ffn0 = jnp.zeros((bp, FFN_CONV - 1, D_FF), hp.dtype)
        hp, kp, vp, sp, cp, fp = hybrid_layer(hp, pos_p, attend_prompt, conv0, ssm0, ffn0, *lw)
        attend_s = functools.partial(attend_paged, cache_k[l], cache_v[l], page_table)
        hs, ksm, vsm, ssm_, csm, fsm = hybrid_layer(hs, pos_s, attend_s, state_conv[l], state_ssm[l],
                                                    state_ffn_conv[l], *lw)
        kp_l.append(kp); vp_l.append(vp); sp_l.append(sp); cp_l.append(cp); fp_l.append(fp)
        ks_l.append(ksm); vs_l.append(vsm); ss_l.append(ssm_); cs_l.append(csm); fs_l.append(fsm)
    return (hp, hs,
            jnp.stack(kp_l), jnp.stack(vp_l), jnp.stack(sp_l), jnp.stack(cp_l), jnp.stack(fp_l),
            jnp.stack(ks_l), jnp.stack(vs_l), jnp.stack(ss_l), jnp.stack(cs_l), jnp.stack(fs_l))
```

```python
import functools

import jax
import jax.numpy as jnp
import numpy as np
from jax import lax
from jax.experimental import pallas as pl
from jax.experimental.pallas import tpu as pltpu

F32 = jnp.float32
BF16 = jnp.bfloat16
HIGHEST = lax.Precision.HIGHEST

D_MODEL = 2048
PAGE_SIZE = 128
N_HEADS = 8
HEAD_DIM = 128
ATTN_WIDTH = N_HEADS * HEAD_DIM
MOBA_BLOCK = 256
MOBA_BLOCK_SHIFT = 8
MOBA_TOPK = 3
ROPE_THETA = 10000.0
SSM_INNER = D_MODEL
SSM_HEAD_DIM = 64
SSM_HEADS = SSM_INNER // SSM_HEAD_DIM
SSM_GROUPS = 4
SSM_GROUP_HEADS = SSM_HEADS // SSM_GROUPS
SSM_GROUP_WIDTH = SSM_INNER // SSM_GROUPS
SSM_STATE = 128
SSM_CONV = 4
SSM_CHUNK = 128
CONV_DIM = SSM_INNER + 2 * SSM_GROUPS * SSM_STATE
D_FF = ((8 * D_MODEL // 3 + 255) // 256) * 256
FFN_CONV = 3
LN_EPS = 1e-5
RMS_EPS = 1e-5

LANES = 128
SUBLANES = 8
VMEM_LIMIT_BYTES = 56 * 1024 * 1024

NEG_BIG = -1e30


def _cparams(*sem):
    return pltpu.CompilerParams(dimension_semantics=sem, vmem_limit_bytes=VMEM_LIMIT_BYTES)


def _tile(n, pref):
    t = min(pref, n)
    while n % t:
        t //= 2
    return t


def _nt_dot(a, b, precision=None):
    return lax.dot_general(a, b, (((1,), (1,)), ((), ())), precision=precision, preferred_element_type=F32)


def _mm_kernel(a_ref, b_ref, o_ref):
    o_ref[...] = jnp.dot(a_ref[...].astype(BF16), b_ref[...], preferred_element_type=F32).astype(o_ref.dtype)


def _matmul(a, b, *, out_dtype=F32, tm=1024, tn=512, name="matmul"):
    m, k = a.shape
    n = b.shape[1]
    tm, tn = _tile(m, tm), _tile(n, tn)
    return pl.pallas_call(
        _mm_kernel,
        grid=(m // tm, n // tn),
        in_specs=[pl.BlockSpec((tm, k), lambda i, j: (i, 0)), pl.BlockSpec((k, tn), lambda i, j: (0, j))],
        out_specs=pl.BlockSpec((tm, tn), lambda i, j: (i, j)),
        out_shape=jax.ShapeDtypeStruct((m, n), out_dtype),
        compiler_params=_cparams("parallel", "arbitrary"),
        name=name,
    )(a, b)


def _mm_rope_kernel(a_ref, b_ref, cos_ref, sin_ref, o_ref):
    acc = jnp.dot(a_ref[...].astype(BF16), b_ref[...], preferred_element_type=F32)
    cos = cos_ref[...]
    sin = sin_ref[...]
    for h in range(acc.shape[1] // HEAD_DIM):
        xh = acc[:, h * HEAD_DIM:(h + 1) * HEAD_DIM]
        o_ref[:, h * HEAD_DIM:(h + 1) * HEAD_DIM] = xh * cos + pltpu.roll(xh, HEAD_DIM // 2, 1) * sin


def _matmul_rope(a, b, cos, sin, *, tm=1024, tn=512, name="matmul_rope"):
    m, k = a.shape
    n = b.shape[1]
    tm, tn = _tile(m, tm), _tile(n, tn)
    return pl.pallas_call(
        _mm_rope_kernel,
        grid=(m // tm, n // tn),
        in_specs=[pl.BlockSpec((tm, k), lambda i, j: (i, 0)), pl.BlockSpec((k, tn), lambda i, j: (0, j)),
                  pl.BlockSpec((tm, HEAD_DIM), lambda i, j: (i, 0)), pl.BlockSpec((tm, HEAD_DIM), lambda i, j: (i, 0))],
        out_specs=pl.BlockSpec((tm, tn), lambda i, j: (i, j)),
        out_shape=jax.ShapeDtypeStruct((m, n), F32),
        compiler_params=_cparams("parallel", "arbitrary"),
        name=name,
    )(a, b, cos, sin)


def _softplus(x):
    return jnp.maximum(x, 0.0) + jnp.log1p(jnp.exp(-jnp.abs(x)))


def _silu(x):
    return x * jax.nn.sigmoid(x)


def _ssd_kernel(xbc_ref, z_ref, dt_ref, cst_ref, h0_ref, cw_ref, cb_ref, dtb_ref, alog_ref, dfull_ref, ng_ref, expand_ref,
                y_ref, hout_ref,
                ext_scr, xc_scr, dt_scr, ht_scr, yd_scr, yoff_scr, st_scr, *, lc, nchunks):
    q = SSM_CHUNK
    n = SSM_STATE
    c = pl.program_id(1)
    halo = SUBLANES

    @pl.when(c == 0)
    def _init():
        if lc < q:
            ext_scr[...] = jnp.zeros(ext_scr.shape, F32)
            dt_scr[...] = jnp.zeros(dt_scr.shape, F32)
        ext_scr[0:halo, :] = cst_ref[0]
        for g in range(SSM_GROUPS):
            ht_scr[g] = h0_ref[0, g].T

    ext_scr[halo:halo + lc, :] = xbc_ref[0]

    slab = 512
    row = lax.broadcasted_iota(jnp.int32, (q, slab), 0)
    for s in range(CONV_DIM // slab):
        cols = slice(s * slab, (s + 1) * slab)
        acc = ext_scr[halo - 3:halo - 3 + q, cols] * cw_ref[0:1, cols]
        for kk in range(1, SSM_CONV):
            acc = acc + ext_scr[halo - 3 + kk:halo - 3 + kk + q, cols] * cw_ref[kk:kk + 1, cols]
        act = _silu(acc + cb_ref[:, cols])
        if lc < q:
            act = jnp.where(row < lc, act, 0.0)
        xc_scr[:, cols] = act
    if nchunks > 1:
        ext_scr[0:halo, :] = ext_scr[lc:lc + halo, :]

    dt_new = _softplus(dt_ref[0] + dtb_ref[...])
    if lc < q:
        dt_scr[0:lc, :] = dt_new
        dt = dt_scr[...]
    else:
        dt = dt_new
    a = -jnp.exp(alog_ref[...])
    da = dt * a
    r_io = lax.broadcasted_iota(jnp.int32, (q, q), 0)
    c_io = lax.broadcasted_iota(jnp.int32, (q, q), 1)
    causal = r_io >= c_io
    cs = jnp.dot(causal.astype(F32), da, precision=HIGHEST, preferred_element_type=F32)
    cs_t = cs.T
    dt_t = dt.T
    e_exp = jnp.dot(jnp.exp(cs), expand_ref[...], precision=HIGHEST, preferred_element_type=F32)
    w_t = jnp.exp(cs_t[:, q - 1:q] - cs_t) * dt_t

    for g in range(SSM_GROUPS):
        b_g = xc_scr[:, SSM_INNER + g * n:SSM_INNER + (g + 1) * n]
        c_g = xc_scr[:, SSM_INNER + SSM_GROUPS * n + g * n:SSM_INNER + SSM_GROUPS * n + (g + 1) * n]
        c_bf = c_g.astype(BF16)
        cb = _nt_dot(c_bf, b_g.astype(BF16))
        b_t = b_g.T
        yoff_scr[:, g * SSM_GROUP_WIDTH:(g + 1) * SSM_GROUP_WIDTH] = jnp.dot(
            c_bf, ht_scr[g].astype(BF16), preferred_element_type=F32)
        for r in range(SSM_GROUP_HEADS):
            h = g * SSM_GROUP_HEADS + r
            cols = slice(h * SSM_HEAD_DIM, (h + 1) * SSM_HEAD_DIM)
            diff = cs[:, h:h + 1] - cs_t[h:h + 1, :]
            lmat = jnp.exp(jnp.where(causal, diff, -jnp.inf))
            m_h = (cb * lmat * dt_t[h:h + 1, :]).astype(BF16)
            x_h = xc_scr[:, cols].astype(BF16)
            yd_scr[:, cols] = jnp.dot(m_h, x_h, preferred_element_type=F32)
            bts = (b_t * w_t[h:h + 1, :]).astype(BF16)
            st_scr[:, r * SSM_HEAD_DIM:(r + 1) * SSM_HEAD_DIM] = jnp.dot(bts, x_h, preferred_element_type=F32)
        decay = e_exp[q - 1:q, g * SSM_GROUP_WIDTH:(g + 1) * SSM_GROUP_WIDTH]
        ht_scr[g] = ht_scr[g] * decay + st_scr[...]

    xs = xc_scr[0:lc, 0:SSM_INNER]
    y = yd_scr[0:lc, :] + yoff_scr[0:lc, :] * e_exp[0:lc, :] + dfull_ref[...] * xs
    y = y * _silu(z_ref[0])
    for g in range(SSM_GROUPS):
        cols = slice(g * SSM_GROUP_WIDTH, (g + 1) * SSM_GROUP_WIDTH)
        yg = y[:, cols]
        ms = jnp.mean(jnp.square(yg), axis=-1, keepdims=True)
        y_ref[0, :, cols] = (yg * lax.rsqrt(ms + RMS_EPS) * ng_ref[:, cols]).astype(y_ref.dtype)

    @pl.when(c == nchunks - 1)
    def _fin():
        for g in range(SSM_GROUPS):
            hout_ref[0, g] = ht_scr[g].T


def _ssd(src, dt_raw, conv_state, ssm_state, p, *, xbc_col, z_col, out_dtype):
    bsz, seq, _ = src.shape
    lc = min(seq, SSM_CHUNK)
    assert seq % lc == 0 and lc % SUBLANES == 0
    nchunks = seq // lc
    q = SSM_CHUNK
    const = lambda shape: pl.BlockSpec(shape, lambda b, c: (0,) * len(shape))
    y, h_last = pl.pallas_call(
        functools.partial(_ssd_kernel, lc=lc, nchunks=nchunks),
        grid=(bsz, nchunks),
        in_specs=[
            pl.BlockSpec((1, lc, CONV_DIM), lambda b, c: (b, c, xbc_col)),
            pl.BlockSpec((1, lc, SSM_INNER), lambda b, c: (b, c, z_col)),
            pl.BlockSpec((1, lc, LANES), lambda b, c: (b, c, 0)),
            pl.BlockSpec((1, SUBLANES, CONV_DIM), lambda b, c: (b, 0, 0)),
            pl.BlockSpec((1, SSM_GROUPS, SSM_GROUP_WIDTH, SSM_STATE), lambda b, c: (b, 0, 0, 0)),
            const((SSM_CONV, CONV_DIM)), const((1, CONV_DIM)), const((1, LANES)), const((1, LANES)),
            const((1, SSM_INNER)), const((1, SSM_INNER)), const((LANES, SSM_INNER)),
        ],
        out_specs=[
            pl.BlockSpec((1, lc, SSM_INNER), lambda b, c: (b, c, 0)),
            pl.BlockSpec((1, SSM_GROUPS, SSM_GROUP_WIDTH, SSM_STATE), lambda b, c: (b, 0, 0, 0)),
        ],
        out_shape=[jax.ShapeDtypeStruct((bsz, seq, SSM_INNER), out_dtype),
                   jax.ShapeDtypeStruct((bsz, SSM_GROUPS, SSM_GROUP_WIDTH, SSM_STATE), F32)],
        scratch_shapes=[
            pltpu.VMEM((SUBLANES + q, CONV_DIM), F32),
            pltpu.VMEM((q, CONV_DIM), F32),
            pltpu.VMEM((q, LANES), F32),
            pltpu.VMEM((SSM_GROUPS, SSM_STATE, SSM_GROUP_WIDTH), F32),
            pltpu.VMEM((q, SSM_INNER), F32),
            pltpu.VMEM((q, SSM_INNER), F32),
            pltpu.VMEM((SSM_STATE, SSM_GROUP_WIDTH), F32),
        ],
        compiler_params=_cparams("parallel", "arbitrary"),
        name="ssd",
    )(src, src, dt_raw, conv_state, ssm_state, p["conv_w"], p["conv_b"], p["dt_bias"], p["a_log"], p["d_full"],
      p["norm_g"], p["expand"])
    return y, h_last


def _select_blocks(gate, own, n_sel):
    blk = lax.broadcasted_iota(jnp.int32, gate.shape, 1)
    blk_f = blk.astype(F32)
    g = jnp.where(blk < own, gate, -jnp.inf)
    sel = jnp.zeros(gate.shape, F32)
    for t in range(n_sel):
        m = jnp.max(g, axis=1, keepdims=True)
        idx = jnp.min(jnp.where(g == m, blk_f, float(LANES)), axis=1, keepdims=True)
        hit = blk_f == idx
        sel = jnp.where(hit & (own > t), 1.0, sel)
        g = jnp.where(hit, -jnp.inf, g)
    return sel


def _kmean_kernel(k_ref, o_ref):
    rows = k_ref.shape[0]
    o_ref[...] = jnp.sum(k_ref[...].reshape(rows // MOBA_BLOCK, MOBA_BLOCK, k_ref.shape[1]), axis=1) / MOBA_BLOCK


def _block_means(qk, *, k_col, nblk_pad):
    t = qk.shape[0]
    nblk = t // MOBA_BLOCK
    per = _tile(nblk, SUBLANES)
    assert per == SUBLANES or per == nblk
    out = pl.pallas_call(
        _kmean_kernel,
        grid=(nblk // per,),
        in_specs=[pl.BlockSpec((per * MOBA_BLOCK, ATTN_WIDTH), lambda i: (i, k_col))],
        out_specs=pl.BlockSpec((per, ATTN_WIDTH), lambda i: (i, 0)),
        out_shape=jax.ShapeDtypeStruct((nblk, ATTN_WIDTH), F32),
        compiler_params=_cparams("parallel"),
        name="moba_block_means",
    )(qk)
    return jnp.pad(out, ((0, nblk_pad - nblk), (0, 0)))


def _attn_kernel(qi_ref, kj_ref, last_ref, qf_ref, q_ref, k_ref, v_ref, km_ref, o_ref,
                 sel_scr, m_scr, l_scr, acc_scr, *, tq, tk, n_sel, scale):
    p = pl.program_id(1)
    qi = qi_ref[p]
    kj = kj_ref[p]
    q_pos = qi * tq + lax.broadcasted_iota(jnp.int32, (tq, 1), 0)
    own = lax.shift_right_logical(q_pos, MOBA_BLOCK_SHIFT)

    @pl.when(kj == 0)
    def _init():
        m_scr[...] = jnp.full(m_scr.shape, NEG_BIG, F32)
        l_scr[...] = jnp.zeros(l_scr.shape, F32)
        acc_scr[...] = jnp.zeros(acc_scr.shape, F32)
        gate = _nt_dot(qf_ref[...], km_ref[...], precision=HIGHEST)
        sel_scr[...] = _select_blocks(gate, own, n_sel)

    s = _nt_dot(q_ref[...], k_ref[...]) * scale
    blk_row = lax.broadcasted_iota(jnp.int32, (LANES, tk), 0)
    k_pos_full = kj * tk + lax.broadcasted_iota(jnp.int32, (LANES, tk), 1)
    expand = (blk_row == lax.shift_right_logical(k_pos_full, MOBA_BLOCK_SHIFT)).astype(BF16)
    picked = jnp.dot(sel_scr[...].astype(BF16), expand, preferred_element_type=F32)
    k_pos = kj * tk + lax.broadcasted_iota(jnp.int32, (1, tk), 1)
    in_own = (lax.shift_right_logical(k_pos, MOBA_BLOCK_SHIFT) == own) & (k_pos <= q_pos)
    s = jnp.where((picked > 0.5) | in_own, s, NEG_BIG)

    m_old = m_scr[...]
    m_new = jnp.maximum(m_old, jnp.max(s, axis=1, keepdims=True))
    alpha = jnp.exp(m_old - m_new)
    pr = jnp.exp(s - m_new)
    l_scr[...] = alpha * l_scr[...] + jnp.sum(pr, axis=1, keepdims=True)
    acc_scr[...] = alpha * acc_scr[...] + jnp.dot(pr.astype(BF16), v_ref[...], preferred_element_type=F32)
    m_scr[...] = m_new

    @pl.when(last_ref[p] == 1)
    def _fin():
        o_ref[...] = (acc_scr[...] / l_scr[...]).astype(o_ref.dtype)


def _attend_prompt(qk, q_bf, k_bf, v_bf, *, out_dtype):
    t = qk.shape[0]
    assert t % MOBA_BLOCK == 0
    nblk = t // MOBA_BLOCK
    assert nblk <= LANES
    n_sel = min(MOBA_TOPK, nblk)
    kmean = _block_means(qk, k_col=1, nblk_pad=LANES)
    tq = _tile(t, 512)
    tk = _tile(t, 1024)
    pairs = [(i, j) for i in range(t // tq) for j in range((i * tq + tq - 1) // tk + 1)]
    qi_arr = jnp.asarray([a for a, _ in pairs], jnp.int32)
    kj_arr = jnp.asarray([b for _, b in pairs], jnp.int32)
    last_arr = jnp.asarray([int(j == (i * tq + tq - 1) // tk) for i, j in pairs], jnp.int32)
    grid_spec = pltpu.PrefetchScalarGridSpec(
        num_scalar_prefetch=3,
        grid=(N_HEADS, len(pairs)),
        in_specs=[
            pl.BlockSpec((tq, HEAD_DIM), lambda h, p, qi, kj, la: (qi[p], h)),
            pl.BlockSpec((tq, HEAD_DIM), lambda h, p, qi, kj, la: (qi[p], h)),
            pl.BlockSpec((tk, HEAD_DIM), lambda h, p, qi, kj, la: (kj[p], h)),
            pl.BlockSpec((tk, HEAD_DIM), lambda h, p, qi, kj, la: (kj[p], h)),
            pl.BlockSpec((LANES, HEAD_DIM), lambda h, p, qi, kj, la: (0, h)),
        ],
        out_specs=pl.BlockSpec((tq, HEAD_DIM), lambda h, p, qi, kj, la: (qi[p], h)),
        scratch_shapes=[pltpu.VMEM((tq, LANES), F32), pltpu.VMEM((tq, 1), F32), pltpu.VMEM((tq, 1), F32),
                        pltpu.VMEM((tq, HEAD_DIM), F32)],
    )
    return pl.pallas_call(
        functools.partial(_attn_kernel, tq=tq, tk=tk, n_sel=n_sel, scale=HEAD_DIM ** -0.5),
        grid_spec=grid_spec,
        out_shape=jax.ShapeDtypeStruct((t, ATTN_WIDTH), out_dtype),
        compiler_params=_cparams("parallel", "arbitrary"),
        name="moba_prompt_attention",
    )(qi_arr, kj_arr, last_arr, qk, q_bf, k_bf, v_bf, kmean)


def _paged_attn_kernel(pt_ref, *refs, gp, n_steps, dec, past, n_sel, scale):
    k_refs = refs[0:gp]
    v_refs = refs[gp:2 * gp]
    q_ref, kn_ref, vn_ref, o_ref = refs[2 * gp:2 * gp + 4]
    qbd_scr, kbf_scr, s_scr, ksum_scr, sel_scr, p_scr, l_scr, oacc_scr, new_scr = refs[2 * gp + 4:]
    ph = pl.program_id(1)
    st = pl.program_id(2)
    page = PAGE_SIZE
    step_keys = gp * page
    t_past = past
    cols = N_HEADS * dec
    own = past // MOBA_BLOCK

    @pl.when((ph == 0) & (st == 0))
    def _init():
        new_scr[...] = jnp.zeros(new_scr.shape, F32)
        new_scr[0:dec, :] = q_ref[...]
        for h in range(N_HEADS):
            qt = new_scr[:, h * HEAD_DIM:(h + 1) * HEAD_DIM].T
            if h:
                qt = pltpu.roll(qt, h * dec, 1)
            qbd_scr[h * HEAD_DIM:(h + 1) * HEAD_DIM, :] = qt.astype(BF16)
        ksum_scr[...] = jnp.zeros(ksum_scr.shape, F32)

    @pl.when(ph == 0)
    def _scores():
        pages_per_blk = MOBA_BLOCK // page
        blks_per_step = gp // pages_per_blk
        sums = []
        for g in range(0, gp, pages_per_blk):
            tot = None
            for gg in range(g, g + pages_per_blk):
                kp = k_refs[gg][0]
                kbf_scr[gg * page:(gg + 1) * page, :] = kp.astype(BF16)
                part = jnp.sum(kp, axis=0, keepdims=True)
                tot = part if tot is None else tot + part
            sums.append(tot)
        for s_static in range(n_steps):
            @pl.when(st == s_static)
            def _store_sums(s_static=s_static):
                for j, tot in enumerate(sums):
                    blk = s_static * blks_per_step + j
                    ksum_scr[blk:blk + 1, :] = tot
        s_t = jnp.dot(kbf_scr[...], qbd_scr[...], preferred_element_type=F32) * scale
        s_scr[pl.ds(pl.multiple_of(st * step_keys, step_keys), step_keys), :] = s_t

    @pl.when((ph == 0) & (st == n_steps - 1))
    def _softmax():
        new_scr[...] = jnp.zeros(new_scr.shape, F32)
        new_scr[0:dec, :] = kn_ref[...]
        s_scr[t_past:t_past + page, :] = jnp.dot(new_scr[...].astype(BF16), qbd_scr[...],
                                                 preferred_element_type=F32) * scale
        own_col = jnp.full((dec, 1), own, jnp.int32)
        sel_scr[...] = jnp.zeros(sel_scr.shape, F32)
        for h in range(N_HEADS):
            kmean = ksum_scr[:, h * HEAD_DIM:(h + 1) * HEAD_DIM] / MOBA_BLOCK
            gate = _nt_dot(q_ref[:, h * HEAD_DIM:(h + 1) * HEAD_DIM], kmean, precision=HIGHEST)
            sel_scr[h * dec:(h + 1) * dec, :] = _select_blocks(gate, own_col, n_sel)
        sel_bf = sel_scr[...].astype(BF16)
        n_chunks = (t_past + page) // page
        col = lax.broadcasted_iota(jnp.int32, (page, LANES), 1)
        krow = lax.broadcasted_iota(jnp.int32, (page, LANES), 0)
        qq = col % dec
        new_ok = jnp.where((krow <= qq) & (col < cols), 1.0, 0.0)

        def allowed_chunk(ci):
            blk_of_key = lax.shift_right_logical(ci * page + krow, MOBA_BLOCK_SHIFT)
            onehot = (blk_of_key == col).astype(BF16)
            picked = _nt_dot(onehot, sel_bf)
            return jnp.where(ci == n_chunks - 1, new_ok, picked) > 0.5

        def max_body(ci, m):
            sc = s_scr[pl.ds(pl.multiple_of(ci * page, page), page), :]
            sc = jnp.where(allowed_chunk(ci), sc, NEG_BIG)
            return jnp.maximum(m, jnp.max(sc, axis=0, keepdims=True))

        m = lax.fori_loop(0, n_chunks, max_body, jnp.full((1, LANES), NEG_BIG, F32))

        def p_body(ci, carry):
            sc = s_scr[pl.ds(pl.multiple_of(ci * page, page), page), :]
            pr = jnp.where(allowed_chunk(ci), jnp.exp(sc - m), 0.0)
            p_scr[ci] = pr.T
            return carry

        lax.fori_loop(0, n_chunks, p_body, 0)
        l_scr[...] = jnp.zeros(l_scr.shape, F32)
        oacc_scr[...] = jnp.zeros(oacc_scr.shape, F32)

    @pl.when(ph == 1)
    def _values():
        for g in range(gp):
            kbf_scr[g * page:(g + 1) * page, :] = v_refs[g][0].astype(BF16)
        pr = jnp.concatenate([p_scr[st * gp + g] for g in range(gp)], axis=1)
        l_scr[...] += jnp.sum(pr, axis=1, keepdims=True)
        oacc_scr[...] += jnp.dot(pr.astype(BF16), kbf_scr[...], preferred_element_type=F32)

    @pl.when((ph == 1) & (st == n_steps - 1))
    def _fin():
        new_scr[...] = jnp.zeros(new_scr.shape, F32)
        new_scr[0:dec, :] = vn_ref[...]
        pr = p_scr[t_past // page]
        l_tot = l_scr[...] + jnp.sum(pr, axis=1, keepdims=True)
        o_full = oacc_scr[...] + jnp.dot(pr.astype(BF16), new_scr[...].astype(BF16), preferred_element_type=F32)
        for h in range(N_HEADS):
            o_ref[:, h * HEAD_DIM:(h + 1) * HEAD_DIM] = (
                o_full[h * dec:(h + 1) * dec, h * HEAD_DIM:(h + 1) * HEAD_DIM] / l_tot[h * dec:(h + 1) * dec, :]
            ).astype(o_ref.dtype)


def _attend_paged(cache_k, cache_v, page_table, qk, v_new, *, dec):
    bsz, n_pages = page_table.shape
    past = n_pages * PAGE_SIZE
    assert past % MOBA_BLOCK == 0 and dec == SUBLANES and N_HEADS * dec <= LANES
    own = past // MOBA_BLOCK
    assert own + 1 <= LANES
    n_sel = min(MOBA_TOPK, own + 1)
    n_pool = cache_k.shape[0]
    ck = cache_k.reshape(n_pool, PAGE_SIZE, ATTN_WIDTH)
    cv = cache_v.reshape(n_pool, PAGE_SIZE, ATTN_WIDTH)
    gp = _tile(n_pages, 8)
    assert gp % (MOBA_BLOCK // PAGE_SIZE) == 0
    n_steps = n_pages // gp
    t_all = past + PAGE_SIZE

    def k_map(g):
        return lambda b, ph, st, pt: (pt[b, jnp.where(ph == 0, st, n_steps - 1) * gp + g], 0, 0)

    def v_map(g):
        return lambda b, ph, st, pt: (pt[b, jnp.where(ph == 1, st, 0) * gp + g], 0, 0)

    page_spec = lambda fn: pl.BlockSpec((1, PAGE_SIZE, ATTN_WIDTH), fn)
    row_spec = lambda col: pl.BlockSpec((dec, ATTN_WIDTH), lambda b, ph, st, pt: (b, col))
    grid_spec = pltpu.PrefetchScalarGridSpec(
        num_scalar_prefetch=1,
        grid=(bsz, 2, n_steps),
        in_specs=[page_spec(k_map(g)) for g in range(gp)] + [page_spec(v_map(g)) for g in range(gp)]
        + [row_spec(0), row_spec(1), row_spec(0)],
        out_specs=pl.BlockSpec((dec, ATTN_WIDTH), lambda b, ph, st, pt: (b, 0)),
        scratch_shapes=[
            pltpu.VMEM((ATTN_WIDTH, LANES), BF16),
            pltpu.VMEM((gp * PAGE_SIZE, ATTN_WIDTH), BF16),
            pltpu.VMEM((t_all, LANES), F32),
            pltpu.VMEM((LANES, ATTN_WIDTH), F32),
            pltpu.VMEM((LANES, LANES), F32),
            pltpu.VMEM((t_all // PAGE_SIZE, LANES, PAGE_SIZE), F32),
            pltpu.VMEM((LANES, 1), F32),
            pltpu.VMEM((LANES, ATTN_WIDTH), F32),
            pltpu.VMEM((PAGE_SIZE, ATTN_WIDTH), F32),
        ],
    )
    return pl.pallas_call(
        functools.partial(_paged_attn_kernel, gp=gp, n_steps=n_steps, dec=dec, past=past, n_sel=n_sel,
                          scale=HEAD_DIM ** -0.5),
        grid_spec=grid_spec,
        out_shape=jax.ShapeDtypeStruct((bsz * dec, ATTN_WIDTH), F32),
        compiler_params=_cparams("parallel", "arbitrary", "arbitrary"),
        name="moba_paged_attention",
    )(page_table, *([ck] * gp), *([cv] * gp), qk, qk, v_new)


def _merge_kernel(oa_ref, ys_ref, wa_ref, ws_ref, ga_ref, gs_ref, o_ref):
    pa = jnp.dot(oa_ref[...].astype(BF16), wa_ref[...], preferred_element_type=F32)
    ps = jnp.dot(ys_ref[...].astype(BF16), ws_ref[...], preferred_element_type=F32)
    o_ref[...] = (jax.nn.sigmoid(ga_ref[...]) * pa + jax.nn.sigmoid(gs_ref[...]) * ps).astype(o_ref.dtype)


def _merge(o_attn, y_s, w_pa, w_ps, rest, *, ga_col, gs_col, tm=512, tn=512):
    m = o_attn.shape[0]
    tm = _tile(m, tm)
    return pl.pallas_call(
        _merge_kernel,
        grid=(m // tm, D_MODEL // tn),
        in_specs=[
            pl.BlockSpec((tm, ATTN_WIDTH), lambda i, j: (i, 0)),
            pl.BlockSpec((tm, SSM_INNER), lambda i, j: (i, 0)),
            pl.BlockSpec((ATTN_WIDTH, tn), lambda i, j: (0, j)),
            pl.BlockSpec((SSM_INNER, tn), lambda i, j: (0, j)),
            pl.BlockSpec((tm, tn), lambda i, j: (i, ga_col * (D_MODEL // tn) + j)),
            pl.BlockSpec((tm, tn), lambda i, j: (i, gs_col * (D_MODEL // tn) + j)),
        ],
        out_specs=pl.BlockSpec((tm, tn), lambda i, j: (i, j)),
        out_shape=jax.ShapeDtypeStruct((m, D_MODEL), BF16),
        compiler_params=_cparams("parallel", "arbitrary"),
        name="gated_merge",
    )(o_attn, y_s, w_pa, w_ps, rest, rest)


def _layer_norm(x, g, b):
    mu = jnp.mean(x, axis=-1, keepdims=True)
    xc = x - mu
    var = jnp.mean(jnp.square(xc), axis=-1, keepdims=True)
    return xc * lax.rsqrt(var + LN_EPS) * g + b


def _proj_ln_kernel(a_ref, w_ref, res_ref, g_ref, b_ref, o_ref, obf_ref, *, alpha):
    acc = jnp.dot(a_ref[...], w_ref[...], preferred_element_type=F32)
    y = _layer_norm(alpha * res_ref[...] + acc, g_ref[...], b_ref[...])
    o_ref[...] = y
    obf_ref[...] = y.astype(BF16)


def _proj_ln(a, w, res, g, b, *, alpha, tm=256):
    m, k = a.shape
    n = w.shape[1]
    tm = _tile(m, tm)
    return pl.pallas_call(
        functools.partial(_proj_ln_kernel, alpha=alpha),
        grid=(m // tm,),
        in_specs=[pl.BlockSpec((tm, k), lambda i: (i, 0)), pl.BlockSpec((k, n), lambda i: (0, 0)),
                  pl.BlockSpec((tm, n), lambda i: (i, 0)), pl.BlockSpec((1, n), lambda i: (0, 0)),
                  pl.BlockSpec((1, n), lambda i: (0, 0))],
        out_specs=[pl.BlockSpec((tm, n), lambda i: (i, 0)), pl.BlockSpec((tm, n), lambda i: (i, 0))],
        out_shape=[jax.ShapeDtypeStruct((m, n), F32), jax.ShapeDtypeStruct((m, n), BF16)],
        compiler_params=_cparams("parallel"),
        name="out_proj_layernorm",
    )(a, w, res, g, b)


def _gelu(x):
    return 0.5 * x * (1.0 + lax.erf(x * np.float32(np.sqrt(0.5))))


def _glu_long_kernel(h_ref, wg_ref, wu_ref, st_ref, cw_ref, cb_ref, act_ref, tail_ref, ext_scr, *, tm, n_row_tiles):
    i = pl.program_id(1)
    halo = SUBLANES
    hh = h_ref[...]
    g_up = jnp.dot(hh, wg_ref[...], preferred_element_type=F32)
    u = jnp.dot(hh, wu_ref[...], preferred_element_type=F32)

    @pl.when(i == 0)
    def _init():
        ext_scr[0:halo, :] = st_ref[...]

    ext_scr[halo:halo + tm, :] = g_up
    acc = ext_scr[halo - 2:halo - 2 + tm, :] * cw_ref[0:1, :]
    for kk in range(1, FFN_CONV):
        acc = acc + ext_scr[halo - 2 + kk:halo - 2 + kk + tm, :] * cw_ref[kk:kk + 1, :]
    act_ref[...] = (_gelu(acc + cb_ref[...]) * u).astype(act_ref.dtype)
    ext_scr[0:halo, :] = ext_scr[tm:tm + halo, :]

    @pl.when(i == n_row_tiles - 1)
    def _tail():
        tail_ref[...] = ext_scr[tm:tm + halo, :]


def _glu_long(h_bf, w_gate, w_up, state, conv_w, conv_b, *, tm=512, tn=512):
    m, k = h_bf.shape
    tm = _tile(m, tm)
    n_row_tiles = m // tm
    return pl.pallas_call(
        functools.partial(_glu_long_kernel, tm=tm, n_row_tiles=n_row_tiles),
        grid=(D_FF // tn, n_row_tiles),
        in_specs=[pl.BlockSpec((tm, k), lambda j, i: (i, 0)), pl.BlockSpec((k, tn), lambda j, i: (0, j)),
                  pl.BlockSpec((k, tn), lambda j, i: (0, j)), pl.BlockSpec((SUBLANES, tn), lambda j, i: (0, j)),
                  pl.BlockSpec((FFN_CONV, tn), lambda j, i: (0, j)), pl.BlockSpec((1, tn), lambda j, i: (0, j))],
        out_specs=[pl.BlockSpec((tm, tn), lambda j, i: (i, j)), pl.BlockSpec((SUBLANES, tn), lambda j, i: (0, j))],
        out_shape=[jax.ShapeDtypeStruct((m, D_FF), BF16), jax.ShapeDtypeStruct((SUBLANES, D_FF), F32)],
        scratch_shapes=[pltpu.VMEM((SUBLANES + tm, tn), F32)],
        compiler_params=_cparams("parallel", "arbitrary"),
        name="convglu_up_long",
    )(h_bf, w_gate, w_up, state, conv_w, conv_b)


def _glu_short_kernel(h_ref, wg_ref, wu_ref, p1_ref, p2_ref, cw_ref, cb_ref, act_ref, gup_ref, *, seq):
    hh = h_ref[...]
    g_up = jnp.dot(hh, wg_ref[...], preferred_element_type=F32)
    u = jnp.dot(hh, wu_ref[...], preferred_element_type=F32)
    gup_ref[...] = g_up
    r = lax.broadcasted_iota(jnp.int32, g_up.shape, 0) % seq
    back1 = jnp.where(r >= 1, pltpu.roll(g_up, 1, 0), p1_ref[...])
    back2 = jnp.where(r >= 2, pltpu.roll(g_up, 2, 0), p2_ref[...])
    acc = back2 * cw_ref[0:1, :] + back1 * cw_ref[1:2, :] + g_up * cw_ref[2:3, :]
    act_ref[...] = (_gelu(acc + cb_ref[...]) * u).astype(act_ref.dtype)


def _glu_short(h_bf, w_gate, w_up, p1, p2, conv_w, conv_b, *, seq, tn=512):
    m, k = h_bf.shape
    full = lambda j: (0, j)
    return pl.pallas_call(
        functools.partial(_glu_short_kernel, seq=seq),
        grid=(D_FF // tn,),
        in_specs=[pl.BlockSpec((m, k), lambda j: (0, 0)), pl.BlockSpec((k, tn), full), pl.BlockSpec((k, tn), full),
                  pl.BlockSpec((m, tn), full), pl.BlockSpec((m, tn), full),
                  pl.BlockSpec((FFN_CONV, tn), full), pl.BlockSpec((1, tn), full)],
        out_specs=[pl.BlockSpec((m, tn), full), pl.BlockSpec((m, tn), full)],
        out_shape=[jax.ShapeDtypeStruct((m, D_FF), BF16), jax.ShapeDtypeStruct((m, D_FF), F32)],
        compiler_params=_cparams("parallel"),
        name="convglu_up_short",
    )(h_bf, w_gate, w_up, p1, p2, conv_w, conv_b)


def _down_ln_kernel(a_ref, w_ref, res_ref, g_ref, b_ref, o_ref, acc_scr, *, alpha, n_k):
    kk = pl.program_id(1)

    @pl.when(kk == 0)
    def _init():
        acc_scr[...] = jnp.zeros(acc_scr.shape, F32)

    acc_scr[...] += jnp.dot(a_ref[...], w_ref[...], preferred_element_type=F32)

    @pl.when(kk == n_k - 1)
    def _fin():
        o_ref[...] = _layer_norm(alpha * res_ref[...] + acc_scr[...], g_ref[...], b_ref[...])


def _down_ln(a, w, res, g, b, *, alpha, tm=512, tk=512):
    m, k = a.shape
    n = w.shape[1]
    tm = _tile(m, tm)
    n_k = k // tk
    return pl.pallas_call(
        functools.partial(_down_ln_kernel, alpha=alpha, n_k=n_k),
        grid=(m // tm, n_k),
        in_specs=[pl.BlockSpec((tm, tk), lambda i, kk: (i, kk)), pl.BlockSpec((tk, n), lambda i, kk: (kk, 0)),
                  pl.BlockSpec((tm, n), lambda i, kk: (i, 0)), pl.BlockSpec((1, n), lambda i, kk: (0, 0)),
                  pl.BlockSpec((1, n), lambda i, kk: (0, 0))],
        out_specs=pl.BlockSpec((tm, n), lambda i, kk: (i, 0)),
        out_shape=jax.ShapeDtypeStruct((m, n), F32),
        scratch_shapes=[pltpu.VMEM((tm, n), F32)],
        compiler_params=_cparams("parallel", "arbitrary"),
        name="down_proj_layernorm",
    )(a, w, res, g, b)


REST_Z, REST_GA, REST_GS = 0, 1, 2
REST_XBC = 3 * D_MODEL // CONV_DIM
REST_V = (3 * D_MODEL + CONV_DIM) // ATTN_WIDTH
assert REST_XBC * CONV_DIM == 3 * D_MODEL and REST_V * ATTN_WIDTH == 3 * D_MODEL + CONV_DIM


def _prep_layer(w_in, ssm_conv_w, ssm_conv_b, dt_bias, a_log, d_skip, ssm_norm_g, w_proj_attn, w_proj_ssm, w_out,
                ln1_g, ln1_b, w_gate, w_up, ffn_conv_w, ffn_conv_b, w_down, ln2_g, ln2_b):
    widths = (ATTN_WIDTH, ATTN_WIDTH, ATTN_WIDTH, SSM_INNER, CONV_DIM, SSM_HEADS, D_MODEL, D_MODEL)
    offs = np.concatenate([[0], np.cumsum(widths)])
    wq, wk, wv, wz, wxbc, wdt, wga, wgs = (w_in[:, offs[i]:offs[i + 1]] for i in range(8))
    row = lambda v: v.reshape(1, -1).astype(F32)
    pad_lanes = lambda v: jnp.pad(row(v), ((0, 0), (0, LANES - v.shape[-1])))
    return {
        "w_qk": jnp.concatenate([wq, wk], axis=1).astype(BF16),
        "w_rest": jnp.concatenate([wz, wga, wgs, wxbc, wv], axis=1).astype(BF16),
        "w_dt": jnp.pad(wdt, ((0, 0), (0, LANES - SSM_HEADS))).astype(BF16),
        "conv_w": ssm_conv_w.astype(F32), "conv_b": row(ssm_conv_b),
        "dt_bias": pad_lanes(dt_bias), "a_log": pad_lanes(a_log),
        "d_full": row(jnp.repeat(d_skip, SSM_HEAD_DIM)), "norm_g": row(ssm_norm_g),
        "expand": (jnp.arange(LANES)[:, None] == jnp.arange(SSM_INNER)[None, :] // SSM_HEAD_DIM).astype(F32),
        "w_pa": w_proj_attn.astype(BF16), "w_ps": w_proj_ssm.astype(BF16), "w_out": w_out.astype(BF16),
        "ln1_g": row(ln1_g), "ln1_b": row(ln1_b),
        "w_gate": w_gate.astype(BF16), "w_up": w_up.astype(BF16),
        "ffn_conv_w": ffn_conv_w.astype(F32), "ffn_conv_b": row(ffn_conv_b),
        "w_down": w_down.astype(BF16), "ln2_g": row(ln2_g), "ln2_b": row(ln2_b),
    }


def _rope_tables(pos):
    half = HEAD_DIM // 2
    inv = ROPE_THETA ** (-jnp.arange(half, dtype=F32) * 2.0 / HEAD_DIM)
    ang = pos.astype(F32)[:, None] * inv[None, :]
    cos, sin = jnp.cos(ang), jnp.sin(ang)
    return jnp.concatenate([cos, cos], axis=1), jnp.concatenate([-sin, sin], axis=1)


def _layer(x, pos, p, *, conv_state, ssm_state, ffn_state, paged, alpha):
    bsz, seq, _ = x.shape
    m = bsz * seq
    x2 = x.reshape(m, D_MODEL)
    x_bf = x2.astype(BF16)
    cos, sin = _rope_tables(pos)
    if bsz > 1:
        cos, sin = jnp.tile(cos, (bsz, 1)), jnp.tile(sin, (bsz, 1))
    qk = _matmul_rope(x_bf, p["w_qk"], cos, sin, name="in_proj_qk_rope")
    rest = _matmul(x_bf, p["w_rest"], name="in_proj_rest")
    dt_raw = _matmul(x_bf, p["w_dt"], tn=LANES, name="in_proj_dt")

    k_new = qk[:, ATTN_WIDTH:]
    v_new = rest[:, REST_V * ATTN_WIDTH:]
    xbc = rest[:, REST_XBC * CONV_DIM:(REST_XBC + 1) * CONV_DIM].reshape(bsz, seq, CONV_DIM)
    new_conv = xbc[:, seq - (SSM_CONV - 1):] if seq >= SSM_CONV - 1 else jnp.concatenate(
        [conv_state, xbc], axis=1)[:, -(SSM_CONV - 1):]

    if paged is None:
        assert bsz == 1
        o_attn = _attend_prompt(qk, qk[:, :ATTN_WIDTH].astype(BF16), k_new.astype(BF16), v_new.astype(BF16),
                                out_dtype=BF16)
    else:
        cache_k, cache_v, page_table = paged
        o_attn = _attend_paged(cache_k, cache_v, page_table, qk, v_new, dec=seq)

    cst = jnp.pad(conv_state.astype(F32), ((0, 0), (SUBLANES - (SSM_CONV - 1), 0), (0, 0)))
    h0 = ssm_state.astype(F32).reshape(bsz, SSM_GROUPS, SSM_GROUP_WIDTH, SSM_STATE)
    y_s, h_last = _ssd(rest.reshape(bsz, seq, -1), dt_raw.reshape(bsz, seq, LANES), cst, h0, p,
                       xbc_col=REST_XBC, z_col=REST_Z, out_dtype=BF16 if seq % 16 == 0 else F32)
    h_new = h_last.reshape(bsz, SSM_HEADS, SSM_HEAD_DIM, SSM_STATE)

    merged = _merge(o_attn, y_s.reshape(m, SSM_INNER), p["w_pa"], p["w_ps"], rest, ga_col=REST_GA, gs_col=REST_GS)
    h1, h1_bf = _proj_ln(merged, p["w_out"], x2, p["ln1_g"], p["ln1_b"], alpha=alpha)

    if bsz == 1:
        st = jnp.pad(ffn_state[0].astype(F32), ((SUBLANES - (FFN_CONV - 1), 0), (0, 0)))
        act, tail = _glu_long(h1_bf, p["w_gate"], p["w_up"], st, p["ffn_conv_w"], p["ffn_conv_b"])
        assert seq >= FFN_CONV - 1
        new_ffn = tail[None, SUBLANES - (FFN_CONV - 1):]
    else:
        assert seq == SUBLANES
        fs = ffn_state.astype(F32)
        p1 = jnp.pad(fs[:, 1:2], ((0, 0), (0, seq - 1), (0, 0))).reshape(m, D_FF)
        p2 = jnp.pad(fs, ((0, 0), (0, seq - 2), (0, 0))).reshape(m, D_FF)
        act, g_up = _glu_short(h1_bf, p["w_gate"], p["w_up"], p1, p2, p["ffn_conv_w"], p["ffn_conv_b"], seq=seq)
        new_ffn = g_up.reshape(bsz, seq, D_FF)[:, seq - (FFN_CONV - 1):]
    y = _down_ln(act, p["w_down"], h1, p["ln2_g"], p["ln2_b"], alpha=alpha)
    return (y.reshape(bsz, seq, D_MODEL), k_new.reshape(bsz, seq, N_HEADS, HEAD_DIM),
            v_new.reshape(bsz, seq, N_HEADS, HEAD_DIM), h_new, new_conv, new_ffn)


def kernel(x_prompt, x_sample, cache_k, cache_v, state_ssm, state_conv, state_ffn_conv, page_table, w_in, ssm_conv_w, ssm_conv_b, dt_bias, a_log, d_skip, ssm_norm_g, w_proj_attn, w_proj_ssm, w_out, ln1_g, ln1_b, w_gate, w_up, ffn_conv_w, ffn_conv_b, w_down, ln2_g, ln2_b):
    depth = w_in.shape[0]
    alpha = (2.0 * depth) ** 0.25
    bp, lp = x_prompt.shape[0], x_prompt.shape[1]
    past_len = page_table.shape[1] * PAGE_SIZE
    pos_p = jnp.arange(lp, dtype=jnp.int32)
    pos_s = past_len + jnp.arange(x_sample.shape[1], dtype=jnp.int32)
    hp, hs = x_prompt, x_sample
    outs_p, outs_s = [], []
    for l in range(depth):
        p = _prep_layer(w_in[l], ssm_conv_w[l], ssm_conv_b[l], dt_bias[l], a_log[l], d_skip[l], ssm_norm_g[l],
                        w_proj_attn[l], w_proj_ssm[l], w_out[l], ln1_g[l], ln1_b[l], w_gate[l], w_up[l],
                        ffn_conv_w[l], ffn_conv_b[l], w_down[l], ln2_g[l], ln2_b[l])
        conv0 = jnp.zeros((bp, SSM_CONV - 1, CONV_DIM), F32)
        ssm0 = jnp.zeros((bp, SSM_HEADS, SSM_HEAD_DIM, SSM_STATE), F32)
        ffn0 = jnp.zeros((bp, FFN_CONV - 1, D_FF), F32)
        hp, *rest_p = _layer(hp, pos_p, p, conv_state=conv0, ssm_state=ssm0, ffn_state=ffn0, paged=None, alpha=alpha)
        hs, *rest_s = _layer(hs, pos_s, p, conv_state=state_conv[l], ssm_state=state_ssm[l],
                             ffn_state=state_ffn_conv[l], paged=(cache_k[l], cache_v[l], page_table), alpha=alpha)
        outs_p.append(rest_p)
        outs_s.append(rest_s)
    stack = lambda outs, i: jnp.stack([o[i] for o in outs])
    return (hp, hs, *(stack(outs_p, i) for i in range(5)), *(stack(outs_s, i) for i in range(5)))
```

```python
import functools

import jax
import jax.numpy as jnp
import numpy as np
from jax import lax
from jax.experimental import pallas as pl
from jax.experimental.pallas import tpu as pltpu

F32 = jnp.float32
BF16 = jnp.bfloat16
HIGHEST = lax.Precision.HIGHEST

D_MODEL = 2048
PAGE_SIZE = 128
N_HEADS = 8
HEAD_DIM = 128
ATTN_WIDTH = N_HEADS * HEAD_DIM
MOBA_BLOCK = 256
MOBA_BLOCK_SHIFT = 8
MOBA_TOPK = 3
ROPE_THETA = 10000.0
SSM_INNER = D_MODEL
SSM_HEAD_DIM = 64
SSM_HEADS = SSM_INNER // SSM_HEAD_DIM
SSM_GROUPS = 4
SSM_GROUP_HEADS = SSM_HEADS // SSM_GROUPS
SSM_GROUP_WIDTH = SSM_INNER // SSM_GROUPS
SSM_STATE = 128
SSM_CONV = 4
SSM_CHUNK = 128
CONV_DIM = SSM_INNER + 2 * SSM_GROUPS * SSM_STATE
D_FF = ((8 * D_MODEL // 3 + 255) // 256) * 256
FFN_CONV = 3
LN_EPS = 1e-5
RMS_EPS = 1e-5

LANES = 128
SUBLANES = 8
VMEM_LIMIT_BYTES = 56 * 1024 * 1024

NEG_BIG = -1e30


def _cparams(*sem):
    return pltpu.CompilerParams(dimension_semantics=sem, vmem_limit_bytes=VMEM_LIMIT_BYTES)


def _tile(n, pref):
    t = min(pref, n)
    while n % t:
        t //= 2
    return t


def _nt_dot(a, b, precision=None):
    return lax.dot_general(a, b, (((1,), (1,)), ((), ())), precision=precision, preferred_element_type=F32)


def _proj_kernel(*refs, rope, with_bf16):
    a_ref, b_ref = refs[0:2]
    outs = refs[4:] if rope else refs[2:]
    acc = jnp.dot(a_ref[...], b_ref[...], preferred_element_type=F32)
    if rope:
        cos = refs[2][...]
        sin = refs[3][...]
    for h in range(acc.shape[1] // HEAD_DIM):
        cols = slice(h * HEAD_DIM, (h + 1) * HEAD_DIM)
        val = acc[:, cols]
        if rope:
            val = val * cos + pltpu.roll(val, HEAD_DIM // 2, 1) * sin
        outs[0][:, cols] = val
        if with_bf16:
            outs[1][:, cols] = val.astype(BF16)


def _project(a, b, rope=None, *, with_bf16=False, tm=1024, tn=512, name):
    m, k = a.shape
    n = b.shape[1]
    tm, tn = _tile(m, tm), _tile(n, tn)
    assert tn % HEAD_DIM == 0
    in_specs = [pl.BlockSpec((tm, k), lambda i, j: (i, 0)), pl.BlockSpec((k, tn), lambda i, j: (0, j))]
    if rope is not None:
        in_specs += [pl.BlockSpec((tm, HEAD_DIM), lambda i, j: (i, 0))] * 2
    out_spec = pl.BlockSpec((tm, tn), lambda i, j: (i, j))
    out = pl.pallas_call(
        functools.partial(_proj_kernel, rope=rope is not None, with_bf16=with_bf16),
        grid=(m // tm, n // tn),
        in_specs=in_specs,
        out_specs=[out_spec] * (2 if with_bf16 else 1),
        out_shape=[jax.ShapeDtypeStruct((m, n), F32)] + ([jax.ShapeDtypeStruct((m, n), BF16)] if with_bf16 else []),
        compiler_params=_cparams("parallel", "arbitrary"),
        name=name,
    )(a, b, *(rope or ()))
    return out if with_bf16 else out[0]


def _softplus(x):
    return jnp.maximum(x, 0.0) + jnp.log1p(jnp.exp(-jnp.abs(x)))


def _silu(x):
    return x * jax.nn.sigmoid(x)


def _ssd_kernel(xbc_ref, z_ref, dt_ref, cst_ref, h0_ref, cw_ref, cb_ref, dtb_ref, alog_ref, dfull_ref, ng_ref, expand_ref,
                y_ref, hout_ref,
                ext_scr, xc_scr, dt_scr, ht_scr, yd_scr, yoff_scr, st_scr, *, lc, nchunks):
    q = SSM_CHUNK
    n = SSM_STATE
    c = pl.program_id(1)
    halo = SUBLANES

    @pl.when(c == 0)
    def _init():
        if lc < q:
            ext_scr[...] = jnp.zeros(ext_scr.shape, F32)
            dt_scr[...] = jnp.zeros(dt_scr.shape, F32)
        ext_scr[0:halo, :] = cst_ref[0]
        for g in range(SSM_GROUPS):
            ht_scr[g] = h0_ref[0, g].T

    ext_scr[halo:halo + lc, :] = xbc_ref[0]

    slab = 512
    row = lax.broadcasted_iota(jnp.int32, (q, slab), 0)
    for s in range(CONV_DIM // slab):
        cols = slice(s * slab, (s + 1) * slab)
        acc = ext_scr[halo - 3:halo - 3 + q, cols] * cw_ref[0:1, cols]
        for kk in range(1, SSM_CONV):
            acc = acc + ext_scr[halo - 3 + kk:halo - 3 + kk + q, cols] * cw_ref[kk:kk + 1, cols]
        act = _silu(acc + cb_ref[:, cols])
        if lc < q:
            act = jnp.where(row < lc, act, 0.0)
        xc_scr[:, cols] = act
    if nchunks > 1:
        ext_scr[0:halo, :] = ext_scr[lc:lc + halo, :]

    dt_new = _softplus(dt_ref[0] + dtb_ref[...])
    if lc < q:
        dt_scr[0:lc, :] = dt_new
        dt = dt_scr[...]
    else:
        dt = dt_new
    a = -jnp.exp(alog_ref[...])
    da = dt * a
    r_io = lax.broadcasted_iota(jnp.int32, (q, q), 0)
    c_io = lax.broadcasted_iota(jnp.int32, (q, q), 1)
    causal = r_io >= c_io
    cs = jnp.dot(causal.astype(F32), da, precision=HIGHEST, preferred_element_type=F32)
    cs_t = cs.T
    dt_t = dt.T
    e_exp = jnp.dot(jnp.exp(cs), expand_ref[...], precision=HIGHEST, preferred_element_type=F32)
    w_t = jnp.exp(cs_t[:, q - 1:q] - cs_t) * dt_t

    for g in range(SSM_GROUPS):
        b_g = xc_scr[:, SSM_INNER + g * n:SSM_INNER + (g + 1) * n]
        c_g = xc_scr[:, SSM_INNER + SSM_GROUPS * n + g * n:SSM_INNER + SSM_GROUPS * n + (g + 1) * n]
        c_bf = c_g.astype(BF16)
        cb = _nt_dot(c_bf, b_g.astype(BF16))
        b_t = b_g.T
        yoff_scr[:, g * SSM_GROUP_WIDTH:(g + 1) * SSM_GROUP_WIDTH] = jnp.dot(
            c_bf, ht_scr[g].astype(BF16), preferred_element_type=F32)
        for r in range(SSM_GROUP_HEADS):
            h = g * SSM_GROUP_HEADS + r
            cols = slice(h * SSM_HEAD_DIM, (h + 1) * SSM_HEAD_DIM)
            diff = cs[:, h:h + 1] - cs_t[h:h + 1, :]
            lmat = jnp.exp(jnp.where(causal, diff, -jnp.inf))
            m_h = (cb * lmat * dt_t[h:h + 1, :]).astype(BF16)
            x_h = xc_scr[:, cols].astype(BF16)
            yd_scr[:, cols] = jnp.dot(m_h, x_h, preferred_element_type=F32)
            bts = (b_t * w_t[h:h + 1, :]).astype(BF16)
            st_scr[:, r * SSM_HEAD_DIM:(r + 1) * SSM_HEAD_DIM] = jnp.dot(bts, x_h, preferred_element_type=F32)
        decay = e_exp[q - 1:q, g * SSM_GROUP_WIDTH:(g + 1) * SSM_GROUP_WIDTH]
        ht_scr[g] = ht_scr[g] * decay + st_scr[...]

    xs = xc_scr[0:lc, 0:SSM_INNER]
    y = yd_scr[0:lc, :] + yoff_scr[0:lc, :] * e_exp[0:lc, :] + dfull_ref[...] * xs
    y = y * _silu(z_ref[0])
    for g in range(SSM_GROUPS):
        cols = slice(g * SSM_GROUP_WIDTH, (g + 1) * SSM_GROUP_WIDTH)
        yg = y[:, cols]
        ms = jnp.mean(jnp.square(yg), axis=-1, keepdims=True)
        y_ref[0, :, cols] = (yg * lax.rsqrt(ms + RMS_EPS) * ng_ref[:, cols]).astype(y_ref.dtype)

    @pl.when(c == nchunks - 1)
    def _fin():
        for g in range(SSM_GROUPS):
            hout_ref[0, g] = ht_scr[g].T


def _ssd(src, dt_raw, conv_state, ssm_state, p, *, xbc_col, z_col, out_dtype):
    bsz, seq, _ = src.shape
    lc = min(seq, SSM_CHUNK)
    assert seq % lc == 0 and lc % SUBLANES == 0
    nchunks = seq // lc
    q = SSM_CHUNK
    const = lambda shape: pl.BlockSpec(shape, lambda b, c: (0,) * len(shape))
    y, h_last = pl.pallas_call(
        functools.partial(_ssd_kernel, lc=lc, nchunks=nchunks),
        grid=(bsz, nchunks),
        in_specs=[
            pl.BlockSpec((1, lc, CONV_DIM), lambda b, c: (b, c, xbc_col)),
            pl.BlockSpec((1, lc, SSM_INNER), lambda b, c: (b, c, z_col)),
            pl.BlockSpec((1, lc, LANES), lambda b, c: (b, c, 0)),
            pl.BlockSpec((1, SUBLANES, CONV_DIM), lambda b, c: (b, 0, 0)),
            pl.BlockSpec((1, SSM_GROUPS, SSM_GROUP_WIDTH, SSM_STATE), lambda b, c: (b, 0, 0, 0)),
            const((SSM_CONV, CONV_DIM)), const((1, CONV_DIM)), const((1, LANES)), const((1, LANES)),
            const((1, SSM_INNER)), const((1, SSM_INNER)), const((LANES, SSM_INNER)),
        ],
        out_specs=[
            pl.BlockSpec((1, lc, SSM_INNER), lambda b, c: (b, c, 0)),
            pl.BlockSpec((1, SSM_GROUPS, SSM_GROUP_WIDTH, SSM_STATE), lambda b, c: (b, 0, 0, 0)),
        ],
        out_shape=[jax.ShapeDtypeStruct((bsz, seq, SSM_INNER), out_dtype),
                   jax.ShapeDtypeStruct((bsz, SSM_GROUPS, SSM_GROUP_WIDTH, SSM_STATE), F32)],
        scratch_shapes=[
            pltpu.VMEM((SUBLANES + q, CONV_DIM), F32),
            pltpu.VMEM((q, CONV_DIM), F32),
            pltpu.VMEM((q, LANES), F32),
            pltpu.VMEM((SSM_GROUPS, SSM_STATE, SSM_GROUP_WIDTH), F32),
            pltpu.VMEM((q, SSM_INNER), F32),
            pltpu.VMEM((q, SSM_INNER), F32),
            pltpu.VMEM((SSM_STATE, SSM_GROUP_WIDTH), F32),
        ],
        compiler_params=_cparams("parallel", "arbitrary"),
        name="ssd",
    )(src, src, dt_raw, conv_state, ssm_state, p["conv_w"], p["conv_b"], p["dt_bias"], p["a_log"], p["d_full"],
      p["norm_g"], p["expand"])
    return y, h_last


def _select_blocks(gate, own, n_sel):
    blk = lax.broadcasted_iota(jnp.int32, gate.shape, 1)
    blk_f = blk.astype(F32)
    g = jnp.where(blk < own, gate, -jnp.inf)
    sel = jnp.zeros(gate.shape, F32)
    for t in range(n_sel):
        m = jnp.max(g, axis=1, keepdims=True)
        idx = jnp.min(jnp.where(g == m, blk_f, float(LANES)), axis=1, keepdims=True)
        hit = blk_f == idx
        sel = jnp.where(hit & (own > t), 1.0, sel)
        g = jnp.where(hit, -jnp.inf, g)
    return sel


def _kmean_kernel(k_ref, o_ref):
    rows = k_ref.shape[0]
    o_ref[...] = jnp.sum(k_ref[...].reshape(rows // MOBA_BLOCK, MOBA_BLOCK, k_ref.shape[1]), axis=1) / MOBA_BLOCK


def _block_means(k, *, nblk_pad):
    t = k.shape[0]
    nblk = t // MOBA_BLOCK
    per = _tile(nblk, SUBLANES)
    assert per == SUBLANES or per == nblk
    out = pl.pallas_call(
        _kmean_kernel,
        grid=(nblk // per,),
        in_specs=[pl.BlockSpec((per * MOBA_BLOCK, ATTN_WIDTH), lambda i: (i, 0))],
        out_specs=pl.BlockSpec((per, ATTN_WIDTH), lambda i: (i, 0)),
        out_shape=jax.ShapeDtypeStruct((nblk, ATTN_WIDTH), F32),
        compiler_params=_cparams("parallel"),
        name="moba_block_means",
    )(k)
    return jnp.pad(out, ((0, nblk_pad - nblk), (0, 0)))


def _attn_kernel(qi_ref, kj_ref, qf_ref, k_ref, koh_ref, v_ref, km_ref, o_ref,
                 qa_scr, m_scr, l_scr, acc_scr, *, tile, n_sel, scale):
    p = pl.program_id(1)
    qi = qi_ref[p]
    kj = kj_ref[p]
    q_pos = qi * tile + lax.broadcasted_iota(jnp.int32, (tile, 1), 0)
    own = lax.shift_right_logical(q_pos, MOBA_BLOCK_SHIFT)

    @pl.when(kj == 0)
    def _init():
        m_scr[...] = jnp.full(m_scr.shape, NEG_BIG, F32)
        l_scr[...] = jnp.zeros(l_scr.shape, F32)
        acc_scr[...] = jnp.zeros(acc_scr.shape, F32)
        qf = qf_ref[...]
        gate = _nt_dot(qf, km_ref[...], precision=HIGHEST)
        sel = _select_blocks(gate, own, n_sel)
        blk = lax.broadcasted_iota(jnp.int32, sel.shape, 1)
        qa_scr[:, 0:HEAD_DIM] = (qf * scale).astype(BF16)
        qa_scr[:, HEAD_DIM:] = jnp.where((sel > 0.5) | (blk == own), 0.0, NEG_BIG).astype(BF16)

    def update(s):
        m_old = m_scr[...]
        m_new = jnp.maximum(m_old, jnp.max(s, axis=1, keepdims=True))
        alpha = jnp.exp(m_old - m_new)
        pr = jnp.exp(s - m_new)
        l_scr[...] = alpha * l_scr[...] + jnp.sum(pr, axis=1, keepdims=True)
        acc_scr[...] = alpha * acc_scr[...] + jnp.dot(pr.astype(BF16), v_ref[...], preferred_element_type=F32)
        m_scr[...] = m_new

    k_aug = jnp.concatenate([k_ref[...], koh_ref[...]], axis=1)
    s = _nt_dot(qa_scr[...], k_aug)

    @pl.when(kj < qi)
    def _past():
        update(s)

    @pl.when(kj == qi)
    def _diag():
        k_pos = kj * tile + lax.broadcasted_iota(jnp.int32, (1, tile), 1)
        future = (lax.shift_right_logical(k_pos, MOBA_BLOCK_SHIFT) == own) & (k_pos > q_pos)
        update(jnp.where(future, NEG_BIG, s))
        o_ref[...] = (acc_scr[...] / l_scr[...]).astype(o_ref.dtype)


def _attend_prompt(q, k, k_bf, v_bf, *, out_dtype):
    t = q.shape[0]
    assert t % MOBA_BLOCK == 0
    nblk = t // MOBA_BLOCK
    assert nblk <= LANES
    n_sel = min(MOBA_TOPK, nblk)
    kmean = _block_means(k, nblk_pad=LANES)
    tile = _tile(t, 1024)
    assert tile % MOBA_BLOCK == 0
    pairs = [(i, j) for i in range(t // tile) for j in range(i + 1)]
    qi_arr = jnp.asarray([a for a, _ in pairs], jnp.int32)
    kj_arr = jnp.asarray([b for _, b in pairs], jnp.int32)
    k_onehot = (jnp.arange(t)[:, None] // MOBA_BLOCK == jnp.arange(LANES)[None, :]).astype(BF16)
    grid_spec = pltpu.PrefetchScalarGridSpec(
        num_scalar_prefetch=2,
        grid=(N_HEADS, len(pairs)),
        in_specs=[
            pl.BlockSpec((tile, HEAD_DIM), lambda h, p, qi, kj: (qi[p], h)),
            pl.BlockSpec((tile, HEAD_DIM), lambda h, p, qi, kj: (kj[p], h)),
            pl.BlockSpec((tile, LANES), lambda h, p, qi, kj: (kj[p], 0)),
            pl.BlockSpec((tile, HEAD_DIM), lambda h, p, qi, kj: (kj[p], h)),
            pl.BlockSpec((LANES, HEAD_DIM), lambda h, p, qi, kj: (0, h)),
        ],
        out_specs=pl.BlockSpec((tile, HEAD_DIM), lambda h, p, qi, kj: (qi[p], h)),
        scratch_shapes=[pltpu.VMEM((tile, 2 * HEAD_DIM), BF16), pltpu.VMEM((tile, 1), F32),
                        pltpu.VMEM((tile, 1), F32), pltpu.VMEM((tile, HEAD_DIM), F32)],
    )
    return pl.pallas_call(
        functools.partial(_attn_kernel, tile=tile, n_sel=n_sel, scale=HEAD_DIM ** -0.5),
        grid_spec=grid_spec,
        out_shape=jax.ShapeDtypeStruct((t, ATTN_WIDTH), out_dtype),
        compiler_params=_cparams("parallel", "arbitrary"),
        name="moba_prompt_attention",
    )(qi_arr, kj_arr, q, k_bf, k_onehot, v_bf, kmean)


def _paged_attn_kernel(pt_ref, *refs, gp, n_steps, dec, past, n_sel, scale):
    k_refs = refs[0:gp]
    v_refs = refs[gp:2 * gp]
    q_ref, kn_ref, vn_ref, o_ref = refs[2 * gp:2 * gp + 4]
    qbd_scr, kbf_scr, s_scr, ksum_scr, sel_scr, l_scr, oacc_scr, new_scr = refs[2 * gp + 4:]
    ph = pl.program_id(1)
    st = pl.program_id(2)
    page = PAGE_SIZE
    step_keys = gp * page
    t_past = past
    cols = N_HEADS * dec
    own = past // MOBA_BLOCK

    @pl.when((ph == 0) & (st == 0))
    def _init():
        new_scr[...] = jnp.zeros(new_scr.shape, F32)
        new_scr[0:dec, :] = q_ref[...]
        for h in range(N_HEADS):
            qt = new_scr[:, h * HEAD_DIM:(h + 1) * HEAD_DIM].T
            if h:
                qt = pltpu.roll(qt, h * dec, 1)
            qbd_scr[h * HEAD_DIM:(h + 1) * HEAD_DIM, :] = qt.astype(BF16)
        ksum_scr[...] = jnp.zeros(ksum_scr.shape, F32)

    @pl.when(ph == 0)
    def _scores():
        pages_per_blk = MOBA_BLOCK // page
        blks_per_step = gp // pages_per_blk
        sums = []
        for g in range(0, gp, pages_per_blk):
            tot = None
            for gg in range(g, g + pages_per_blk):
                kp = k_refs[gg][0]
                kbf_scr[gg * page:(gg + 1) * page, :] = kp.astype(BF16)
                part = jnp.sum(kp, axis=0, keepdims=True)
                tot = part if tot is None else tot + part
            sums.append(tot)
        for s_static in range(n_steps):
            @pl.when(st == s_static)
            def _store_sums(s_static=s_static):
                for j, tot in enumerate(sums):
                    blk = s_static * blks_per_step + j
                    ksum_scr[blk:blk + 1, :] = tot
        s_t = jnp.dot(kbf_scr[...], qbd_scr[...], preferred_element_type=F32) * scale
        s_scr[pl.ds(pl.multiple_of(st * step_keys, step_keys), step_keys), :] = s_t

    @pl.when((ph == 0) & (st == n_steps - 1))
    def _softmax():
        new_scr[...] = jnp.zeros(new_scr.shape, F32)
        new_scr[0:dec, :] = kn_ref[...]
        s_scr[t_past:t_past + page, :] = jnp.dot(new_scr[...].astype(BF16), qbd_scr[...],
                                                 preferred_element_type=F32) * scale
        own_col = jnp.full((dec, 1), own, jnp.int32)
        sel_scr[...] = jnp.zeros(sel_scr.shape, F32)
        for h in range(N_HEADS):
            kmean = ksum_scr[:, h * HEAD_DIM:(h + 1) * HEAD_DIM] / MOBA_BLOCK
            gate = _nt_dot(q_ref[:, h * HEAD_DIM:(h + 1) * HEAD_DIM], kmean, precision=HIGHEST)
            sel_scr[h * dec:(h + 1) * dec, :] = _select_blocks(gate, own_col, n_sel)
        sel_bf = sel_scr[...].astype(BF16)
        col = lax.broadcasted_iota(jnp.int32, (page, LANES), 1)
        krow = lax.broadcasted_iota(jnp.int32, (page, LANES), 0)
        new_ok = (krow <= col % dec) & (col < cols)
        s_new = jnp.where(new_ok, s_scr[t_past:t_past + page, :], NEG_BIG)
        s_scr[t_past:t_past + page, :] = s_new
        blk_col = lax.broadcasted_iota(jnp.int32, (step_keys, LANES), 1)
        key_row = lax.broadcasted_iota(jnp.int32, (step_keys, LANES), 0)

        def mask_body(ci, m):
            rows = pl.ds(pl.multiple_of(ci * step_keys, step_keys), step_keys)
            onehot = (lax.shift_right_logical(ci * step_keys + key_row, MOBA_BLOCK_SHIFT) == blk_col).astype(BF16)
            picked = _nt_dot(onehot, sel_bf)
            sc = jnp.where(picked > 0.5, s_scr[rows, :], NEG_BIG)
            s_scr[rows, :] = sc
            return jnp.maximum(m, jnp.max(sc, axis=0, keepdims=True))

        m = lax.fori_loop(0, n_steps, mask_body, jnp.max(s_new, axis=0, keepdims=True))

        def exp_body(ci, carry):
            rows = pl.ds(pl.multiple_of(ci * step_keys, step_keys), step_keys)
            s_scr[rows, :] = jnp.exp(s_scr[rows, :] - m)
            return carry

        lax.fori_loop(0, n_steps, exp_body, 0)
        s_scr[t_past:t_past + page, :] = jnp.exp(s_new - m)
        l_scr[...] = jnp.zeros(l_scr.shape, F32)
        oacc_scr[...] = jnp.zeros(oacc_scr.shape, F32)

    @pl.when(ph == 1)
    def _values():
        for g in range(gp):
            kbf_scr[g * page:(g + 1) * page, :] = v_refs[g][0].astype(BF16)
        pr = s_scr[pl.ds(pl.multiple_of(st * step_keys, step_keys), step_keys), :].T
        l_scr[...] += jnp.sum(pr, axis=1, keepdims=True)
        oacc_scr[...] += jnp.dot(pr.astype(BF16), kbf_scr[...], preferred_element_type=F32)

    @pl.when((ph == 1) & (st == n_steps - 1))
    def _fin():
        new_scr[...] = jnp.zeros(new_scr.shape, F32)
        new_scr[0:dec, :] = vn_ref[...]
        pr = s_scr[t_past:t_past + page, :].T
        l_tot = l_scr[...] + jnp.sum(pr, axis=1, keepdims=True)
        o_full = oacc_scr[...] + jnp.dot(pr.astype(BF16), new_scr[...].astype(BF16), preferred_element_type=F32)
        for h in range(N_HEADS):
            o_ref[:, h * HEAD_DIM:(h + 1) * HEAD_DIM] = (
                o_full[h * dec:(h + 1) * dec, h * HEAD_DIM:(h + 1) * HEAD_DIM] / l_tot[h * dec:(h + 1) * dec, :]
            ).astype(o_ref.dtype)


def _attend_paged(cache_k, cache_v, layer, page_table, q, k_new, v_new, *, dec):
    bsz, n_pages = page_table.shape
    past = n_pages * PAGE_SIZE
    assert past % MOBA_BLOCK == 0 and dec == SUBLANES and N_HEADS * dec <= LANES
    own = past // MOBA_BLOCK
    assert own + 1 <= LANES
    n_sel = min(MOBA_TOPK, own + 1)
    depth, n_pool = cache_k.shape[0], cache_k.shape[1]
    ck = cache_k.reshape(depth * n_pool, PAGE_SIZE, ATTN_WIDTH)
    cv = cache_v.reshape(depth * n_pool, PAGE_SIZE, ATTN_WIDTH)
    first = layer * n_pool
    gp = _tile(n_pages, 8)
    assert gp % (MOBA_BLOCK // PAGE_SIZE) == 0
    n_steps = n_pages // gp
    t_all = past + PAGE_SIZE

    def k_map(g):
        return lambda b, ph, st, pt: (first + pt[b, jnp.where(ph == 0, st, n_steps - 1) * gp + g], 0, 0)

    def v_map(g):
        return lambda b, ph, st, pt: (first + pt[b, jnp.where(ph == 1, st, 0) * gp + g], 0, 0)

    page_spec = lambda fn: pl.BlockSpec((1, PAGE_SIZE, ATTN_WIDTH), fn)
    row_spec = pl.BlockSpec((dec, ATTN_WIDTH), lambda b, ph, st, pt: (b, 0))
    grid_spec = pltpu.PrefetchScalarGridSpec(
        num_scalar_prefetch=1,
        grid=(bsz, 2, n_steps),
        in_specs=[page_spec(k_map(g)) for g in range(gp)] + [page_spec(v_map(g)) for g in range(gp)]
        + [row_spec] * 3,
        out_specs=row_spec,
        scratch_shapes=[
            pltpu.VMEM((ATTN_WIDTH, LANES), BF16),
            pltpu.VMEM((gp * PAGE_SIZE, ATTN_WIDTH), BF16),
            pltpu.VMEM((t_all, LANES), F32),
            pltpu.VMEM((LANES, ATTN_WIDTH), F32),
            pltpu.VMEM((LANES, LANES), F32),
            pltpu.VMEM((LANES, 1), F32),
            pltpu.VMEM((LANES, ATTN_WIDTH), F32),
            pltpu.VMEM((PAGE_SIZE, ATTN_WIDTH), F32),
        ],
    )
    return pl.pallas_call(
        functools.partial(_paged_attn_kernel, gp=gp, n_steps=n_steps, dec=dec, past=past, n_sel=n_sel,
                          scale=HEAD_DIM ** -0.5),
        grid_spec=grid_spec,
        out_shape=jax.ShapeDtypeStruct((bsz * dec, ATTN_WIDTH), F32),
        compiler_params=_cparams("parallel", "arbitrary", "arbitrary"),
        name="moba_paged_attention",
    )(page_table, *([ck] * gp), *([cv] * gp), q, k_new, v_new)


def _merge_kernel(oa_ref, ys_ref, wa_ref, ws_ref, ga_ref, gs_ref, o_ref):
    pa = jnp.dot(oa_ref[...].astype(BF16), wa_ref[...], preferred_element_type=F32)
    ps = jnp.dot(ys_ref[...].astype(BF16), ws_ref[...], preferred_element_type=F32)
    o_ref[...] = (jax.nn.sigmoid(ga_ref[...]) * pa + jax.nn.sigmoid(gs_ref[...]) * ps).astype(o_ref.dtype)


def _merge(o_attn, y_s, w_pa, w_ps, rest, *, ga_col, gs_col, tm=512, tn=512):
    m = o_attn.shape[0]
    tm = _tile(m, tm)
    return pl.pallas_call(
        _merge_kernel,
        grid=(m // tm, D_MODEL // tn),
        in_specs=[
            pl.BlockSpec((tm, ATTN_WIDTH), lambda i, j: (i, 0)),
            pl.BlockSpec((tm, SSM_INNER), lambda i, j: (i, 0)),
            pl.BlockSpec((ATTN_WIDTH, tn), lambda i, j: (0, j)),
            pl.BlockSpec((SSM_INNER, tn), lambda i, j: (0, j)),
            pl.BlockSpec((tm, tn), lambda i, j: (i, ga_col * (D_MODEL // tn) + j)),
            pl.BlockSpec((tm, tn), lambda i, j: (i, gs_col * (D_MODEL // tn) + j)),
        ],
        out_specs=pl.BlockSpec((tm, tn), lambda i, j: (i, j)),
        out_shape=jax.ShapeDtypeStruct((m, D_MODEL), BF16),
        compiler_params=_cparams("parallel", "arbitrary"),
        name="gated_merge",
    )(o_attn, y_s, w_pa, w_ps, rest, rest)


def _layer_norm(x, g, b):
    mu = jnp.mean(x, axis=-1, keepdims=True)
    xc = x - mu
    var = jnp.mean(jnp.square(xc), axis=-1, keepdims=True)
    return xc * lax.rsqrt(var + LN_EPS) * g + b


def _proj_ln_kernel(a_ref, w_ref, res_ref, g_ref, b_ref, o_ref, obf_ref, *, alpha):
    acc = jnp.dot(a_ref[...], w_ref[...], preferred_element_type=F32)
    y = _layer_norm(alpha * res_ref[...] + acc, g_ref[...], b_ref[...])
    o_ref[...] = y
    obf_ref[...] = y.astype(BF16)


def _proj_ln(a, w, res, g, b, *, alpha, tm=256):
    m, k = a.shape
    n = w.shape[1]
    tm = _tile(m, tm)
    return pl.pallas_call(
        functools.partial(_proj_ln_kernel, alpha=alpha),
        grid=(m // tm,),
        in_specs=[pl.BlockSpec((tm, k), lambda i: (i, 0)), pl.BlockSpec((k, n), lambda i: (0, 0)),
                  pl.BlockSpec((tm, n), lambda i: (i, 0)), pl.BlockSpec((1, n), lambda i: (0, 0)),
                  pl.BlockSpec((1, n), lambda i: (0, 0))],
        out_specs=[pl.BlockSpec((tm, n), lambda i: (i, 0)), pl.BlockSpec((tm, n), lambda i: (i, 0))],
        out_shape=[jax.ShapeDtypeStruct((m, n), F32), jax.ShapeDtypeStruct((m, n), BF16)],
        compiler_params=_cparams("parallel"),
        name="out_proj_layernorm",
    )(a, w, res, g, b)


def _gelu(x):
    return 0.5 * x * (1.0 + lax.erf(x * np.float32(np.sqrt(0.5))))


def _glu_long_kernel(h_ref, wg_ref, wu_ref, st_ref, cw_ref, cb_ref, act_ref, tail_ref, ext_scr, *, tm, n_row_tiles):
    i = pl.program_id(1)
    halo = SUBLANES
    hh = h_ref[...]
    g_up = jnp.dot(hh, wg_ref[...], preferred_element_type=F32)
    u = jnp.dot(hh, wu_ref[...], preferred_element_type=F32)

    @pl.when(i == 0)
    def _init():
        ext_scr[0:halo, :] = st_ref[...]

    ext_scr[halo:halo + tm, :] = g_up
    acc = ext_scr[halo - 2:halo - 2 + tm, :] * cw_ref[0:1, :]
    for kk in range(1, FFN_CONV):
        acc = acc + ext_scr[halo - 2 + kk:halo - 2 + kk + tm, :] * cw_ref[kk:kk + 1, :]
    act_ref[...] = (_gelu(acc + cb_ref[...]) * u).astype(act_ref.dtype)
    ext_scr[0:halo, :] = ext_scr[tm:tm + halo, :]

    @pl.when(i == n_row_tiles - 1)
    def _tail():
        tail_ref[...] = ext_scr[tm:tm + halo, :]


def _glu_long(h_bf, w_gate, w_up, state, conv_w, conv_b, *, tm=1024, tn=512):
    m, k = h_bf.shape
    tm = _tile(m, tm)
    n_row_tiles = m // tm
    return pl.pallas_call(
        functools.partial(_glu_long_kernel, tm=tm, n_row_tiles=n_row_tiles),
        grid=(D_FF // tn, n_row_tiles),
        in_specs=[pl.BlockSpec((tm, k), lambda j, i: (i, 0)), pl.BlockSpec((k, tn), lambda j, i: (0, j)),
                  pl.BlockSpec((k, tn), lambda j, i: (0, j)), pl.BlockSpec((SUBLANES, tn), lambda j, i: (0, j)),
                  pl.BlockSpec((FFN_CONV, tn), lambda j, i: (0, j)), pl.BlockSpec((1, tn), lambda j, i: (0, j))],
        out_specs=[pl.BlockSpec((tm, tn), lambda j, i: (i, j)), pl.BlockSpec((SUBLANES, tn), lambda j, i: (0, j))],
        out_shape=[jax.ShapeDtypeStruct((m, D_FF), BF16), jax.ShapeDtypeStruct((SUBLANES, D_FF), F32)],
        scratch_shapes=[pltpu.VMEM((SUBLANES + tm, tn), F32)],
        compiler_params=_cparams("parallel", "arbitrary"),
        name="convglu_up_long",
    )(h_bf, w_gate, w_up, state, conv_w, conv_b)


def _glu_short_kernel(h_ref, wg_ref, wu_ref, p1_ref, p2_ref, cw_ref, cb_ref, act_ref, gup_ref, *, seq):
    hh = h_ref[...]
    g_up = jnp.dot(hh, wg_ref[...], preferred_element_type=F32)
    u = jnp.dot(hh, wu_ref[...], preferred_element_type=F32)
    gup_ref[...] = g_up
    r = lax.broadcasted_iota(jnp.int32, g_up.shape, 0) % seq
    back1 = jnp.where(r >= 1, pltpu.roll(g_up, 1, 0), p1_ref[...])
    back2 = jnp.where(r >= 2, pltpu.roll(g_up, 2, 0), p2_ref[...])
    acc = back2 * cw_ref[0:1, :] + back1 * cw_ref[1:2, :] + g_up * cw_ref[2:3, :]
    act_ref[...] = (_gelu(acc + cb_ref[...]) * u).astype(act_ref.dtype)


def _glu_short(h_bf, w_gate, w_up, p1, p2, conv_w, conv_b, *, seq, tn=512):
    m, k = h_bf.shape
    full = lambda j: (0, j)
    return pl.pallas_call(
        functools.partial(_glu_short_kernel, seq=seq),
        grid=(D_FF // tn,),
        in_specs=[pl.BlockSpec((m, k), lambda j: (0, 0)), pl.BlockSpec((k, tn), full), pl.BlockSpec((k, tn), full),
                  pl.BlockSpec((m, tn), full), pl.BlockSpec((m, tn), full),
                  pl.BlockSpec((FFN_CONV, tn), full), pl.BlockSpec((1, tn), full)],
        out_specs=[pl.BlockSpec((m, tn), full), pl.BlockSpec((m, tn), full)],
        out_shape=[jax.ShapeDtypeStruct((m, D_FF), BF16), jax.ShapeDtypeStruct((m, D_FF), F32)],
        compiler_params=_cparams("parallel"),
        name="convglu_up_short",
    )(h_bf, w_gate, w_up, p1, p2, conv_w, conv_b)


def _down_ln_kernel(a_ref, w_ref, res_ref, g_ref, b_ref, o_ref, acc_scr, *, alpha, n_k):
    kk = pl.program_id(1)

    @pl.when(kk == 0)
    def _init():
        acc_scr[...] = jnp.zeros(acc_scr.shape, F32)

    acc_scr[...] += jnp.dot(a_ref[...], w_ref[...], preferred_element_type=F32)

    @pl.when(kk == n_k - 1)
    def _fin():
        o_ref[...] = _layer_norm(alpha * res_ref[...] + acc_scr[...], g_ref[...], b_ref[...])


def _down_ln(a, w, res, g, b, *, alpha, tm=512, tk=D_FF // 4):
    m, k = a.shape
    n = w.shape[1]
    tm = _tile(m, tm)
    n_k = k // tk
    return pl.pallas_call(
        functools.partial(_down_ln_kernel, alpha=alpha, n_k=n_k),
        grid=(m // tm, n_k),
        in_specs=[pl.BlockSpec((tm, tk), lambda i, kk: (i, kk)), pl.BlockSpec((tk, n), lambda i, kk: (kk, 0)),
                  pl.BlockSpec((tm, n), lambda i, kk: (i, 0)), pl.BlockSpec((1, n), lambda i, kk: (0, 0)),
                  pl.BlockSpec((1, n), lambda i, kk: (0, 0))],
        out_specs=pl.BlockSpec((tm, n), lambda i, kk: (i, 0)),
        out_shape=jax.ShapeDtypeStruct((m, n), F32),
        scratch_shapes=[pltpu.VMEM((tm, n), F32)],
        compiler_params=_cparams("parallel", "arbitrary"),
        name="down_proj_layernorm",
    )(a, w, res, g, b)


REST_Z, REST_GA, REST_GS = 0, 1, 2
REST_XBC = 3 * D_MODEL // CONV_DIM
assert REST_XBC * CONV_DIM == 3 * D_MODEL


def _prep_layer(w_in, ssm_conv_w, ssm_conv_b, dt_bias, a_log, d_skip, ssm_norm_g, w_proj_attn, w_proj_ssm, w_out,
                ln1_g, ln1_b, w_gate, w_up, ffn_conv_w, ffn_conv_b, w_down, ln2_g, ln2_b):
    widths = (ATTN_WIDTH, ATTN_WIDTH, ATTN_WIDTH, SSM_INNER, CONV_DIM, SSM_HEADS, D_MODEL, D_MODEL)
    offs = np.concatenate([[0], np.cumsum(widths)])
    wq, wk, wv, wz, wxbc, wdt, wga, wgs = (w_in[:, offs[i]:offs[i + 1]] for i in range(8))
    row = lambda v: v.reshape(1, -1).astype(F32)
    pad_lanes = lambda v: jnp.pad(row(v), ((0, 0), (0, LANES - v.shape[-1])))
    return {
        "w_q": wq.astype(BF16), "w_k": wk.astype(BF16), "w_v": wv.astype(BF16),
        "w_rest": jnp.concatenate([wz, wga, wgs, wxbc], axis=1).astype(BF16),
        "w_dt": jnp.pad(wdt, ((0, 0), (0, LANES - SSM_HEADS))).astype(BF16),
        "conv_w": ssm_conv_w.astype(F32), "conv_b": row(ssm_conv_b),
        "dt_bias": pad_lanes(dt_bias), "a_log": pad_lanes(a_log),
        "d_full": row(jnp.repeat(d_skip, SSM_HEAD_DIM)), "norm_g": row(ssm_norm_g),
        "expand": (jnp.arange(LANES)[:, None] == jnp.arange(SSM_INNER)[None, :] // SSM_HEAD_DIM).astype(F32),
        "w_pa": w_proj_attn.astype(BF16), "w_ps": w_proj_ssm.astype(BF16), "w_out": w_out.astype(BF16),
        "ln1_g": row(ln1_g), "ln1_b": row(ln1_b),
        "w_gate": w_gate.astype(BF16), "w_up": w_up.astype(BF16),
        "ffn_conv_w": ffn_conv_w.astype(F32), "ffn_conv_b": row(ffn_conv_b),
        "w_down": w_down.astype(BF16), "ln2_g": row(ln2_g), "ln2_b": row(ln2_b),
    }


def _rope_tables(pos):
    half = HEAD_DIM // 2
    inv = ROPE_THETA ** (-jnp.arange(half, dtype=F32) * 2.0 / HEAD_DIM)
    ang = pos.astype(F32)[:, None] * inv[None, :]
    cos, sin = jnp.cos(ang), jnp.sin(ang)
    return jnp.concatenate([cos, cos], axis=1), jnp.concatenate([-sin, sin], axis=1)


def _layer(x, pos, p, *, conv_state, ssm_state, ffn_state, paged, alpha):
    bsz, seq, _ = x.shape
    m = bsz * seq
    x2 = x.reshape(m, D_MODEL)
    x_bf = x2.astype(BF16)
    cos, sin = _rope_tables(pos)
    if bsz > 1:
        cos, sin = jnp.tile(cos, (bsz, 1)), jnp.tile(sin, (bsz, 1))
    prompt = paged is None
    q = _project(x_bf, p["w_q"], (cos, sin), name="in_proj_q_rope")
    k_new = _project(x_bf, p["w_k"], (cos, sin), with_bf16=prompt, name="in_proj_k_rope")
    v_new = _project(x_bf, p["w_v"], with_bf16=prompt, name="in_proj_v")
    if prompt:
        (k_new, k_bf), (v_new, v_bf) = k_new, v_new
    rest = _project(x_bf, p["w_rest"], tn=1024, name="in_proj_rest")
    dt_raw = _project(x_bf, p["w_dt"], tn=LANES, name="in_proj_dt")

    xbc = rest[:, REST_XBC * CONV_DIM:(REST_XBC + 1) * CONV_DIM].reshape(bsz, seq, CONV_DIM)
    new_conv = xbc[:, seq - (SSM_CONV - 1):] if seq >= SSM_CONV - 1 else jnp.concatenate(
        [conv_state, xbc], axis=1)[:, -(SSM_CONV - 1):]

    if prompt:
        assert bsz == 1
        o_attn = _attend_prompt(q, k_new, k_bf, v_bf, out_dtype=BF16)
    else:
        cache_k, cache_v, layer, page_table = paged
        o_attn = _attend_paged(cache_k, cache_v, layer, page_table, q, k_new, v_new, dec=seq)

    cst = jnp.pad(conv_state.astype(F32), ((0, 0), (SUBLANES - (SSM_CONV - 1), 0), (0, 0)))
    h0 = ssm_state.astype(F32).reshape(bsz, SSM_GROUPS, SSM_GROUP_WIDTH, SSM_STATE)
    y_s, h_last = _ssd(rest.reshape(bsz, seq, -1), dt_raw.reshape(bsz, seq, LANES), cst, h0, p,
                       xbc_col=REST_XBC, z_col=REST_Z, out_dtype=BF16 if seq % 16 == 0 else F32)
    h_new = h_last.reshape(bsz, SSM_HEADS, SSM_HEAD_DIM, SSM_STATE)

    merged = _merge(o_attn, y_s.reshape(m, SSM_INNER), p["w_pa"], p["w_ps"], rest, ga_col=REST_GA, gs_col=REST_GS)
    h1, h1_bf = _proj_ln(merged, p["w_out"], x2, p["ln1_g"], p["ln1_b"], alpha=alpha)

    if bsz == 1:
        st = jnp.pad(ffn_state[0].astype(F32), ((SUBLANES - (FFN_CONV - 1), 0), (0, 0)))
        act, tail = _glu_long(h1_bf, p["w_gate"], p["w_up"], st, p["ffn_conv_w"], p["ffn_conv_b"])
        assert seq >= FFN_CONV - 1
        new_ffn = tail[None, SUBLANES - (FFN_CONV - 1):]
    else:
        assert seq == SUBLANES
        fs = ffn_state.astype(F32)
        p1 = jnp.pad(fs[:, 1:2], ((0, 0), (0, seq - 1), (0, 0))).reshape(m, D_FF)
        p2 = jnp.pad(fs, ((0, 0), (0, seq - 2), (0, 0))).reshape(m, D_FF)
        act, g_up = _glu_short(h1_bf, p["w_gate"], p["w_up"], p1, p2, p["ffn_conv_w"], p["ffn_conv_b"], seq=seq)
        new_ffn = g_up.reshape(bsz, seq, D_FF)[:, seq - (FFN_CONV - 1):]
    y = _down_ln(act, p["w_down"], h1, p["ln2_g"], p["ln2_b"], alpha=alpha)
    return (y.reshape(bsz, seq, D_MODEL), k_new.reshape(bsz, seq, N_HEADS, HEAD_DIM),
            v_new.reshape(bsz, seq, N_HEADS, HEAD_DIM), h_new, new_conv, new_ffn)


def kernel(x_prompt, x_sample, cache_k, cache_v, state_ssm, state_conv, state_ffn_conv, page_table, w_in, ssm_conv_w, ssm_conv_b, dt_bias, a_log, d_skip, ssm_norm_g, w_proj_attn, w_proj_ssm, w_out, ln1_g, ln1_b, w_gate, w_up, ffn_conv_w, ffn_conv_b, w_down, ln2_g, ln2_b):
    depth = w_in.shape[0]
    alpha = (2.0 * depth) ** 0.25
    bp, lp = x_prompt.shape[0], x_prompt.shape[1]
    past_len = page_table.shape[1] * PAGE_SIZE
    pos_p = jnp.arange(lp, dtype=jnp.int32)
    pos_s = past_len + jnp.arange(x_sample.shape[1], dtype=jnp.int32)
    hp, hs = x_prompt, x_sample
    outs_p, outs_s = [], []
    for l in range(depth):
        p = _prep_layer(w_in[l], ssm_conv_w[l], ssm_conv_b[l], dt_bias[l], a_log[l], d_skip[l], ssm_norm_g[l],
                        w_proj_attn[l], w_proj_ssm[l], w_out[l], ln1_g[l], ln1_b[l], w_gate[l], w_up[l],
                        ffn_conv_w[l], ffn_conv_b[l], w_down[l], ln2_g[l], ln2_b[l])
        conv0 = jnp.zeros((bp, SSM_CONV - 1, CONV_DIM), F32)
        ssm0 = jnp.zeros((bp, SSM_HEADS, SSM_HEAD_DIM, SSM_STATE), F32)
        ffn0 = jnp.zeros((bp, FFN_CONV - 1, D_FF), F32)
        hp, *rest_p = _layer(hp, pos_p, p, conv_state=conv0, ssm_state=ssm0, ffn_state=ffn0, paged=None, alpha=alpha)
        hs, *rest_s = _layer(hs, pos_s, p, conv_state=state_conv[l], ssm_state=state_ssm[l],
                             ffn_state=state_ffn_conv[l], paged=(cache_k, cache_v, l, page_table), alpha=alpha)
        outs_p.append(rest_p)
        outs_s.append(rest_s)
    stack = lambda outs, i: jnp.stack([o[i] for o in outs])
    return (hp, hs, *(stack(outs_p, i) for i in range(5)), *(stack(outs_s, i) for i in range(5)))
```

```python
import functools

import jax
import jax.numpy as jnp
import numpy as np
from jax import lax
from jax.experimental import pallas as pl
from jax.experimental.pallas import tpu as pltpu

F32 = jnp.float32
BF16 = jnp.bfloat16
HIGHEST = lax.Precision.HIGHEST

D_MODEL = 2048
PAGE_SIZE = 128
N_HEADS = 8
HEAD_DIM = 128
ATTN_WIDTH = N_HEADS * HEAD_DIM
MOBA_BLOCK = 256
MOBA_BLOCK_SHIFT = 8
MOBA_TOPK = 3
ROPE_THETA = 10000.0
SSM_INNER = D_MODEL
SSM_HEAD_DIM = 64
SSM_HEADS = SSM_INNER // SSM_HEAD_DIM
SSM_GROUPS = 4
SSM_GROUP_HEADS = SSM_HEADS // SSM_GROUPS
SSM_GROUP_WIDTH = SSM_INNER // SSM_GROUPS
SSM_STATE = 128
SSM_CONV = 4
SSM_CHUNK = 128
CONV_DIM = SSM_INNER + 2 * SSM_GROUPS * SSM_STATE
D_FF = ((8 * D_MODEL // 3 + 255) // 256) * 256
FFN_CONV = 3
LN_EPS = 1e-5
RMS_EPS = 1e-5

LANES = 128
SUBLANES = 8
VMEM_LIMIT_BYTES = 56 * 1024 * 1024

NEG_BIG = -1e30


def _cparams(*sem):
    return pltpu.CompilerParams(dimension_semantics=sem, vmem_limit_bytes=VMEM_LIMIT_BYTES)


def _tile(n, pref):
    t = min(pref, n)
    while n % t:
        t //= 2
    return t


def _nt_dot(a, b, precision=None):
    return lax.dot_general(a, b, (((1,), (1,)), ((), ())), precision=precision, preferred_element_type=F32)


def _proj_kernel(*refs, rope, with_bf16):
    a_ref, b_ref = refs[0:2]
    outs = refs[4:] if rope else refs[2:]
    acc = jnp.dot(a_ref[...], b_ref[...], preferred_element_type=F32)
    if rope:
        cos = refs[2][...]
        sin = refs[3][...]
    for h in range(acc.shape[1] // HEAD_DIM):
        cols = slice(h * HEAD_DIM, (h + 1) * HEAD_DIM)
        val = acc[:, cols]
        if rope:
            val = val * cos + pltpu.roll(val, HEAD_DIM // 2, 1) * sin
        outs[0][:, cols] = val
        if with_bf16:
            outs[1][:, cols] = val.astype(BF16)


def _project(a, b, rope=None, *, with_bf16=False, tm=1024, tn=512, name):
    m, k = a.shape
    n = b.shape[1]
    tm, tn = _tile(m, tm), _tile(n, tn)
    assert tn % HEAD_DIM == 0
    in_specs = [pl.BlockSpec((tm, k), lambda i, j: (i, 0)), pl.BlockSpec((k, tn), lambda i, j: (0, j))]
    if rope is not None:
        in_specs += [pl.BlockSpec((tm, HEAD_DIM), lambda i, j: (i, 0))] * 2
    out_spec = pl.BlockSpec((tm, tn), lambda i, j: (i, j))
    out = pl.pallas_call(
        functools.partial(_proj_kernel, rope=rope is not None, with_bf16=with_bf16),
        grid=(m // tm, n // tn),
        in_specs=in_specs,
        out_specs=[out_spec] * (2 if with_bf16 else 1),
        out_shape=[jax.ShapeDtypeStruct((m, n), F32)] + ([jax.ShapeDtypeStruct((m, n), BF16)] if with_bf16 else []),
        compiler_params=_cparams("parallel", "arbitrary"),
        name=name,
    )(a, b, *(rope or ()))
    return out if with_bf16 else out[0]


def _softplus(x):
    return jnp.maximum(x, 0.0) + jnp.log1p(jnp.exp(-jnp.abs(x)))


def _silu(x):
    return x * jax.nn.sigmoid(x)


def _ssd_kernel(xbc_ref, z_ref, dt_ref, cst_ref, h0_ref, cw_ref, cb_ref, dtb_ref, alog_ref, dfull_ref, ng_ref, expand_ref,
                y_ref, hout_ref,
                ext_scr, xc_scr, dt_scr, ht_scr, yd_scr, yoff_scr, st_scr, *, lc, nchunks):
    q = SSM_CHUNK
    n = SSM_STATE
    c = pl.program_id(1)
    halo = SUBLANES

    @pl.when(c == 0)
    def _init():
        if lc < q:
            ext_scr[...] = jnp.zeros(ext_scr.shape, F32)
            dt_scr[...] = jnp.zeros(dt_scr.shape, F32)
        ext_scr[0:halo, :] = cst_ref[0]
        for g in range(SSM_GROUPS):
            ht_scr[g] = h0_ref[0, g].T

    ext_scr[halo:halo + lc, :] = xbc_ref[0]

    slab = 512
    row = lax.broadcasted_iota(jnp.int32, (q, slab), 0)
    for s in range(CONV_DIM // slab):
        cols = slice(s * slab, (s + 1) * slab)
        acc = ext_scr[halo - 3:halo - 3 + q, cols] * cw_ref[0:1, cols]
        for kk in range(1, SSM_CONV):
            acc = acc + ext_scr[halo - 3 + kk:halo - 3 + kk + q, cols] * cw_ref[kk:kk + 1, cols]
        act = _silu(acc + cb_ref[:, cols])
        if lc < q:
            act = jnp.where(row < lc, act, 0.0)
        xc_scr[:, cols] = act
    if nchunks > 1:
        ext_scr[0:halo, :] = ext_scr[lc:lc + halo, :]

    dt_new = _softplus(dt_ref[0] + dtb_ref[...])
    if lc < q:
        dt_scr[0:lc, :] = dt_new
        dt = dt_scr[...]
    else:
        dt = dt_new
    a = -jnp.exp(alog_ref[...])
    da = dt * a
    r_io = lax.broadcasted_iota(jnp.int32, (q, q), 0)
    c_io = lax.broadcasted_iota(jnp.int32, (q, q), 1)
    causal = r_io >= c_io
    cs = jnp.dot(causal.astype(F32), da, precision=HIGHEST, preferred_element_type=F32)
    cs_t = cs.T
    dt_t = dt.T
    e_exp = jnp.dot(jnp.exp(cs), expand_ref[...], precision=HIGHEST, preferred_element_type=F32)
    w_t = jnp.exp(cs_t[:, q - 1:q] - cs_t) * dt_t

    for g in range(SSM_GROUPS):
        b_g = xc_scr[:, SSM_INNER + g * n:SSM_INNER + (g + 1) * n]
        c_g = xc_scr[:, SSM_INNER + SSM_GROUPS * n + g * n:SSM_INNER + SSM_GROUPS * n + (g + 1) * n]
        c_bf = c_g.astype(BF16)
        cb = _nt_dot(c_bf, b_g.astype(BF16))
        b_t = b_g.T
        yoff_scr[:, g * SSM_GROUP_WIDTH:(g + 1) * SSM_GROUP_WIDTH] = jnp.dot(
            c_bf, ht_scr[g].astype(BF16), preferred_element_type=F32)
        for r in range(SSM_GROUP_HEADS):
            h = g * SSM_GROUP_HEADS + r
            cols = slice(h * SSM_HEAD_DIM, (h + 1) * SSM_HEAD_DIM)
            diff = cs[:, h:h + 1] - cs_t[h:h + 1, :]
            lmat = jnp.exp(jnp.where(causal, diff, -jnp.inf))
            m_h = (cb * lmat * dt_t[h:h + 1, :]).astype(BF16)
            x_h = xc_scr[:, cols].astype(BF16)
            yd_scr[:, cols] = jnp.dot(m_h, x_h, preferred_element_type=F32)
            bts = (b_t * w_t[h:h + 1, :]).astype(BF16)
            st_scr[:, r * SSM_HEAD_DIM:(r + 1) * SSM_HEAD_DIM] = jnp.dot(bts, x_h, preferred_element_type=F32)
        decay = e_exp[q - 1:q, g * SSM_GROUP_WIDTH:(g + 1) * SSM_GROUP_WIDTH]
        ht_scr[g] = ht_scr[g] * decay + st_scr[...]

    xs = xc_scr[0:lc, 0:SSM_INNER]
    y = yd_scr[0:lc, :] + yoff_scr[0:lc, :] * e_exp[0:lc, :] + dfull_ref[...] * xs
    y = y * _silu(z_ref[0])
    for g in range(SSM_GROUPS):
        cols = slice(g * SSM_GROUP_WIDTH, (g + 1) * SSM_GROUP_WIDTH)
        yg = y[:, cols]
        ms = jnp.mean(jnp.square(yg), axis=-1, keepdims=True)
        y_ref[0, :, cols] = (yg * lax.rsqrt(ms + RMS_EPS) * ng_ref[:, cols]).astype(y_ref.dtype)

    @pl.when(c == nchunks - 1)
    def _fin():
        for g in range(SSM_GROUPS):
            hout_ref[0, g] = ht_scr[g].T


def _ssd(src, dt_raw, conv_state, ssm_state, p, *, xbc_col, z_col, out_dtype):
    bsz, seq, _ = src.shape
    lc = min(seq, SSM_CHUNK)
    assert seq % lc == 0 and lc % SUBLANES == 0
    nchunks = seq // lc
    q = SSM_CHUNK
    const = lambda shape: pl.BlockSpec(shape, lambda b, c: (0,) * len(shape))
    y, h_last = pl.pallas_call(
        functools.partial(_ssd_kernel, lc=lc, nchunks=nchunks),
        grid=(bsz, nchunks),
        in_specs=[
            pl.BlockSpec((1, lc, CONV_DIM), lambda b, c: (b, c, xbc_col)),
            pl.BlockSpec((1, lc, SSM_INNER), lambda b, c: (b, c, z_col)),
            pl.BlockSpec((1, lc, LANES), lambda b, c: (b, c, 0)),
            pl.BlockSpec((1, SUBLANES, CONV_DIM), lambda b, c: (b, 0, 0)),
            pl.BlockSpec((1, SSM_GROUPS, SSM_GROUP_WIDTH, SSM_STATE), lambda b, c: (b, 0, 0, 0)),
            const((SSM_CONV, CONV_DIM)), const((1, CONV_DIM)), const((1, LANES)), const((1, LANES)),
            const((1, SSM_INNER)), const((1, SSM_INNER)), const((LANES, SSM_INNER)),
        ],
        out_specs=[
            pl.BlockSpec((1, lc, SSM_INNER), lambda b, c: (b, c, 0)),
            pl.BlockSpec((1, SSM_GROUPS, SSM_GROUP_WIDTH, SSM_STATE), lambda b, c: (b, 0, 0, 0)),
        ],
        out_shape=[jax.ShapeDtypeStruct((bsz, seq, SSM_INNER), out_dtype),
                   jax.ShapeDtypeStruct((bsz, SSM_GROUPS, SSM_GROUP_WIDTH, SSM_STATE), F32)],
        scratch_shapes=[
            pltpu.VMEM((SUBLANES + q, CONV_DIM), F32),
            pltpu.VMEM((q, CONV_DIM), F32),
            pltpu.VMEM((q, LANES), F32),
            pltpu.VMEM((SSM_GROUPS, SSM_STATE, SSM_GROUP_WIDTH), F32),
            pltpu.VMEM((q, SSM_INNER), F32),
            pltpu.VMEM((q, SSM_INNER), F32),
            pltpu.VMEM((SSM_STATE, SSM_GROUP_WIDTH), F32),
        ],
        compiler_params=_cparams("parallel", "arbitrary"),
        name="ssd",
    )(src, src, dt_raw, conv_state, ssm_state, p["conv_w"], p["conv_b"], p["dt_bias"], p["a_log"], p["d_full"],
      p["norm_g"], p["expand"])
    return y, h_last


def _select_blocks(gate, own, n_sel):
    blk = lax.broadcasted_iota(jnp.int32, gate.shape, 1)
    blk_f = blk.astype(F32)
    g = jnp.where(blk < own, gate, -jnp.inf)
    sel = jnp.zeros(gate.shape, F32)
    for t in range(n_sel):
        m = jnp.max(g, axis=1, keepdims=True)
        idx = jnp.min(jnp.where(g == m, blk_f, float(LANES)), axis=1, keepdims=True)
        hit = blk_f == idx
        sel = jnp.where(hit & (own > t), 1.0, sel)
        g = jnp.where(hit, -jnp.inf, g)
    return sel


def _kmean_kernel(k_ref, o_ref):
    rows = k_ref.shape[0]
    o_ref[...] = jnp.sum(k_ref[...].reshape(rows // MOBA_BLOCK, MOBA_BLOCK, k_ref.shape[1]), axis=1) / MOBA_BLOCK


def _block_means(k, *, nblk_pad):
    t = k.shape[0]
    nblk = t // MOBA_BLOCK
    per = _tile(nblk, SUBLANES)
    assert per == SUBLANES or per == nblk
    out = pl.pallas_call(
        _kmean_kernel,
        grid=(nblk // per,),
        in_specs=[pl.BlockSpec((per * MOBA_BLOCK, ATTN_WIDTH), lambda i: (i, 0))],
        out_specs=pl.BlockSpec((per, ATTN_WIDTH), lambda i: (i, 0)),
        out_shape=jax.ShapeDtypeStruct((nblk, ATTN_WIDTH), F32),
        compiler_params=_cparams("parallel"),
        name="moba_block_means",
    )(k)
    return jnp.pad(out, ((0, nblk_pad - nblk), (0, 0)))


def _attn_kernel(qi_ref, kj_ref, qf_ref, k_ref, koh_ref, v_ref, km_ref, o_ref,
                 qa_scr, m_scr, l_scr, acc_scr, *, tile, n_sel, scale):
    p = pl.program_id(1)
    qi = qi_ref[p]
    kj = kj_ref[p]
    q_pos = qi * tile + lax.broadcasted_iota(jnp.int32, (tile, 1), 0)
    own = lax.shift_right_logical(q_pos, MOBA_BLOCK_SHIFT)

    @pl.when(kj == 0)
    def _init():
        m_scr[...] = jnp.full(m_scr.shape, NEG_BIG, F32)
        l_scr[...] = jnp.zeros(l_scr.shape, F32)
        acc_scr[...] = jnp.zeros(acc_scr.shape, F32)
        qf = qf_ref[...]
        gate = _nt_dot(qf, km_ref[...], precision=HIGHEST)
        sel = _select_blocks(gate, own, n_sel)
        blk = lax.broadcasted_iota(jnp.int32, sel.shape, 1)
        qa_scr[:, 0:HEAD_DIM] = (qf * scale).astype(BF16)
        qa_scr[:, HEAD_DIM:] = jnp.where((sel > 0.5) | (blk == own), 0.0, NEG_BIG).astype(BF16)

    def update(s):
        m_old = m_scr[...]
        m_new = jnp.maximum(m_old, jnp.max(s, axis=1, keepdims=True))
        alpha = jnp.exp(m_old - m_new)
        pr = jnp.exp(s - m_new)
        l_scr[...] = alpha * l_scr[...] + jnp.sum(pr, axis=1, keepdims=True)
        acc_scr[...] = alpha * acc_scr[...] + jnp.dot(pr.astype(BF16), v_ref[...], preferred_element_type=F32)
        m_scr[...] = m_new

    k_aug = jnp.concatenate([k_ref[...], koh_ref[...]], axis=1)
    s = _nt_dot(qa_scr[...], k_aug)

    @pl.when(kj < qi)
    def _past():
        update(s)

    @pl.when(kj == qi)
    def _diag():
        k_pos = kj * tile + lax.broadcasted_iota(jnp.int32, (1, tile), 1)
        future = (lax.shift_right_logical(k_pos, MOBA_BLOCK_SHIFT) == own) & (k_pos > q_pos)
        update(jnp.where(future, NEG_BIG, s))
        o_ref[...] = (acc_scr[...] / l_scr[...]).astype(o_ref.dtype)


def _attend_prompt(q, k, k_bf, v_bf, *, out_dtype):
    t = q.shape[0]
    assert t % MOBA_BLOCK == 0
    nblk = t // MOBA_BLOCK
    assert nblk <= LANES
    n_sel = min(MOBA_TOPK, nblk)
    kmean = _block_means(k, nblk_pad=LANES)
    tile = _tile(t, 1024)
    assert tile % MOBA_BLOCK == 0
    pairs = [(i, j) for i in range(t // tile) for j in range(i + 1)]
    qi_arr = jnp.asarray([a for a, _ in pairs], jnp.int32)
    kj_arr = jnp.asarray([b for _, b in pairs], jnp.int32)
    k_onehot = (jnp.arange(t)[:, None] // MOBA_BLOCK == jnp.arange(LANES)[None, :]).astype(BF16)
    grid_spec = pltpu.PrefetchScalarGridSpec(
        num_scalar_prefetch=2,
        grid=(N_HEADS, len(pairs)),
        in_specs=[
            pl.BlockSpec((tile, HEAD_DIM), lambda h, p, qi, kj: (qi[p], h)),
            pl.BlockSpec((tile, HEAD_DIM), lambda h, p, qi, kj: (kj[p], h)),
            pl.BlockSpec((tile, LANES), lambda h, p, qi, kj: (kj[p], 0)),
            pl.BlockSpec((tile, HEAD_DIM), lambda h, p, qi, kj: (kj[p], h)),
            pl.BlockSpec((LANES, HEAD_DIM), lambda h, p, qi, kj: (0, h)),
        ],
        out_specs=pl.BlockSpec((tile, HEAD_DIM), lambda h, p, qi, kj: (qi[p], h)),
        scratch_shapes=[pltpu.VMEM((tile, 2 * HEAD_DIM), BF16), pltpu.VMEM((tile, 1), F32),
                        pltpu.VMEM((tile, 1), F32), pltpu.VMEM((tile, HEAD_DIM), F32)],
    )
    return pl.pallas_call(
        functools.partial(_attn_kernel, tile=tile, n_sel=n_sel, scale=HEAD_DIM ** -0.5),
        grid_spec=grid_spec,
        out_shape=jax.ShapeDtypeStruct((t, ATTN_WIDTH), out_dtype),
        compiler_params=_cparams("parallel", "arbitrary"),
        name="moba_prompt_attention",
    )(qi_arr, kj_arr, q, k_bf, k_onehot, v_bf, kmean)


def _page_head(page_ref, h):
    return page_ref[0, pl.ds(h, PAGE_SIZE, stride=N_HEADS), :]


def _paged_attn_kernel(pt_ref, *refs, gp, n_steps, dec, past, n_sel, scale):
    k_refs = refs[0:gp]
    v_refs = refs[gp:2 * gp]
    q_ref, kn_ref, vn_ref, o_ref = refs[2 * gp:2 * gp + 4]
    qbd_scr, kbf_scr, s_scr, ksum_scr, sel_scr, l_scr, oacc_scr, new_scr = refs[2 * gp + 4:]
    ph = pl.program_id(1)
    st = pl.program_id(2)
    page = PAGE_SIZE
    step_keys = gp * page
    t_past = past
    cols = N_HEADS * dec
    own = past // MOBA_BLOCK

    @pl.when((ph == 0) & (st == 0))
    def _init():
        new_scr[...] = jnp.zeros(new_scr.shape, F32)
        new_scr[0:dec, :] = q_ref[...]
        for h in range(N_HEADS):
            qt = new_scr[:, h * HEAD_DIM:(h + 1) * HEAD_DIM].T
            if h:
                qt = pltpu.roll(qt, h * dec, 1)
            qbd_scr[h * HEAD_DIM:(h + 1) * HEAD_DIM, :] = qt.astype(BF16)
        ksum_scr[...] = jnp.zeros(ksum_scr.shape, F32)

    @pl.when(ph == 0)
    def _scores():
        pages_per_blk = MOBA_BLOCK // page
        blks_per_step = gp // pages_per_blk
        sums = []
        for g in range(0, gp, pages_per_blk):
            for h in range(N_HEADS):
                tot = None
                for gg in range(g, g + pages_per_blk):
                    kh = _page_head(k_refs[gg], h)
                    kbf_scr[gg * page:(gg + 1) * page, h * HEAD_DIM:(h + 1) * HEAD_DIM] = kh.astype(BF16)
                    part = jnp.sum(kh, axis=0, keepdims=True)
                    tot = part if tot is None else tot + part
                sums.append(tot)
        for s_static in range(n_steps):
            @pl.when(st == s_static)
            def _store_sums(s_static=s_static):
                for j, tot in enumerate(sums):
                    blk = s_static * blks_per_step + j // N_HEADS
                    h = j % N_HEADS
                    ksum_scr[blk:blk + 1, h * HEAD_DIM:(h + 1) * HEAD_DIM] = tot
        s_t = jnp.dot(kbf_scr[...], qbd_scr[...], preferred_element_type=F32) * scale
        s_scr[pl.ds(pl.multiple_of(st * step_keys, step_keys), step_keys), :] = s_t

    @pl.when((ph == 0) & (st == n_steps - 1))
    def _softmax():
        new_scr[...] = jnp.zeros(new_scr.shape, F32)
        new_scr[0:dec, :] = kn_ref[...]
        s_scr[t_past:t_past + page, :] = jnp.dot(new_scr[...].astype(BF16), qbd_scr[...],
                                                 preferred_element_type=F32) * scale
        own_col = jnp.full((dec, 1), own, jnp.int32)
        sel_scr[...] = jnp.zeros(sel_scr.shape, F32)
        for h in range(N_HEADS):
            kmean = ksum_scr[:, h * HEAD_DIM:(h + 1) * HEAD_DIM] / MOBA_BLOCK
            gate = _nt_dot(q_ref[:, h * HEAD_DIM:(h + 1) * HEAD_DIM], kmean, precision=HIGHEST)
            sel_scr[h * dec:(h + 1) * dec, :] = _select_blocks(gate, own_col, n_sel)
        sel_bf = sel_scr[...].astype(BF16)
        col = lax.broadcasted_iota(jnp.int32, (page, LANES), 1)
        krow = lax.broadcasted_iota(jnp.int32, (page, LANES), 0)
        new_ok = (krow <= col % dec) & (col < cols)
        s_new = jnp.where(new_ok, s_scr[t_past:t_past + page, :], NEG_BIG)
        s_scr[t_past:t_past + page, :] = s_new
        blk_col = lax.broadcasted_iota(jnp.int32, (step_keys, LANES), 1)
        key_row = lax.broadcasted_iota(jnp.int32, (step_keys, LANES), 0)

        def mask_body(ci, m):
            rows = pl.ds(pl.multiple_of(ci * step_keys, step_keys), step_keys)
            onehot = (lax.shift_right_logical(ci * step_keys + key_row, MOBA_BLOCK_SHIFT) == blk_col).astype(BF16)
            picked = _nt_dot(onehot, sel_bf)
            sc = jnp.where(picked > 0.5, s_scr[rows, :], NEG_BIG)
            s_scr[rows, :] = sc
            return jnp.maximum(m, jnp.max(sc, axis=0, keepdims=True))

        m = lax.fori_loop(0, n_steps, mask_body, jnp.max(s_new, axis=0, keepdims=True))

        def exp_body(ci, carry):
            rows = pl.ds(pl.multiple_of(ci * step_keys, step_keys), step_keys)
            s_scr[rows, :] = jnp.exp(s_scr[rows, :] - m)
            return carry

        lax.fori_loop(0, n_steps, exp_body, 0)
        s_scr[t_past:t_past + page, :] = jnp.exp(s_new - m)
        l_scr[...] = jnp.zeros(l_scr.shape, F32)
        oacc_scr[...] = jnp.zeros(oacc_scr.shape, F32)

    @pl.when(ph == 1)
    def _values():
        for g in range(gp):
            for h in range(N_HEADS):
                kbf_scr[g * page:(g + 1) * page, h * HEAD_DIM:(h + 1) * HEAD_DIM] = _page_head(v_refs[g], h).astype(BF16)
        pr = s_scr[pl.ds(pl.multiple_of(st * step_keys, step_keys), step_keys), :].T
        l_scr[...] += jnp.sum(pr, axis=1, keepdims=True)
        oacc_scr[...] += jnp.dot(pr.astype(BF16), kbf_scr[...], preferred_element_type=F32)

    @pl.when((ph == 1) & (st == n_steps - 1))
    def _fin():
        new_scr[...] = jnp.zeros(new_scr.shape, F32)
        new_scr[0:dec, :] = vn_ref[...]
        pr = s_scr[t_past:t_past + page, :].T
        l_tot = l_scr[...] + jnp.sum(pr, axis=1, keepdims=True)
        o_full = oacc_scr[...] + jnp.dot(pr.astype(BF16), new_scr[...].astype(BF16), preferred_element_type=F32)
        for h in range(N_HEADS):
            o_ref[:, h * HEAD_DIM:(h + 1) * HEAD_DIM] = (
                o_full[h * dec:(h + 1) * dec, h * HEAD_DIM:(h + 1) * HEAD_DIM] / l_tot[h * dec:(h + 1) * dec, :]
            ).astype(o_ref.dtype)


def _attend_paged(cache_k, cache_v, layer, page_table, q, k_new, v_new, *, dec):
    bsz, n_pages = page_table.shape
    past = n_pages * PAGE_SIZE
    assert past % MOBA_BLOCK == 0 and dec == SUBLANES and N_HEADS * dec <= LANES
    own = past // MOBA_BLOCK
    assert own + 1 <= LANES
    n_sel = min(MOBA_TOPK, own + 1)
    depth, n_pool = cache_k.shape[0], cache_k.shape[1]
    ck = cache_k.reshape(depth * n_pool, PAGE_SIZE * N_HEADS, HEAD_DIM)
    cv = cache_v.reshape(depth * n_pool, PAGE_SIZE * N_HEADS, HEAD_DIM)
    first = layer * n_pool
    gp = _tile(n_pages, 8)
    assert gp % (MOBA_BLOCK // PAGE_SIZE) == 0
    n_steps = n_pages // gp
    t_all = past + PAGE_SIZE

    def k_map(g):
        return lambda b, ph, st, pt: (first + pt[b, jnp.where(ph == 0, st, n_steps - 1) * gp + g], 0, 0)

    def v_map(g):
        return lambda b, ph, st, pt: (first + pt[b, jnp.where(ph == 1, st, 0) * gp + g], 0, 0)

    page_spec = lambda fn: pl.BlockSpec((1, PAGE_SIZE * N_HEADS, HEAD_DIM), fn)
    row_spec = pl.BlockSpec((dec, ATTN_WIDTH), lambda b, ph, st, pt: (b, 0))
    grid_spec = pltpu.PrefetchScalarGridSpec(
        num_scalar_prefetch=1,
        grid=(bsz, 2, n_steps),
        in_specs=[page_spec(k_map(g)) for g in range(gp)] + [page_spec(v_map(g)) for g in range(gp)]
        + [row_spec] * 3,
        out_specs=row_spec,
        scratch_shapes=[
            pltpu.VMEM((ATTN_WIDTH, LANES), BF16),
            pltpu.VMEM((gp * PAGE_SIZE, ATTN_WIDTH), BF16),
            pltpu.VMEM((t_all, LANES), F32),
            pltpu.VMEM((LANES, ATTN_WIDTH), F32),
            pltpu.VMEM((LANES, LANES), F32),
            pltpu.VMEM((LANES, 1), F32),
            pltpu.VMEM((LANES, ATTN_WIDTH), F32),
            pltpu.VMEM((PAGE_SIZE, ATTN_WIDTH), F32),
        ],
    )
    return pl.pallas_call(
        functools.partial(_paged_attn_kernel, gp=gp, n_steps=n_steps, dec=dec, past=past, n_sel=n_sel,
                          scale=HEAD_DIM ** -0.5),
        grid_spec=grid_spec,
        out_shape=jax.ShapeDtypeStruct((bsz * dec, ATTN_WIDTH), F32),
        compiler_params=_cparams("parallel", "arbitrary", "arbitrary"),
        name="moba_paged_attention",
    )(page_table, *([ck] * gp), *([cv] * gp), q, k_new, v_new)


def _merge_kernel(oa_ref, ys_ref, wa_ref, ws_ref, ga_ref, gs_ref, o_ref):
    pa = jnp.dot(oa_ref[...].astype(BF16), wa_ref[...], preferred_element_type=F32)
    ps = jnp.dot(ys_ref[...].astype(BF16), ws_ref[...], preferred_element_type=F32)
    o_ref[...] = (jax.nn.sigmoid(ga_ref[...]) * pa + jax.nn.sigmoid(gs_ref[...]) * ps).astype(o_ref.dtype)


def _merge(o_attn, y_s, w_pa, w_ps, rest, *, ga_col, gs_col, tm=512, tn=512):
    m = o_attn.shape[0]
    tm = _tile(m, tm)
    return pl.pallas_call(
        _merge_kernel,
        grid=(m // tm, D_MODEL // tn),
        in_specs=[
            pl.BlockSpec((tm, ATTN_WIDTH), lambda i, j: (i, 0)),
            pl.BlockSpec((tm, SSM_INNER), lambda i, j: (i, 0)),
            pl.BlockSpec((ATTN_WIDTH, tn), lambda i, j: (0, j)),
            pl.BlockSpec((SSM_INNER, tn), lambda i, j: (0, j)),
            pl.BlockSpec((tm, tn), lambda i, j: (i, ga_col * (D_MODEL // tn) + j)),
            pl.BlockSpec((tm, tn), lambda i, j: (i, gs_col * (D_MODEL // tn) + j)),
        ],
        out_specs=pl.BlockSpec((tm, tn), lambda i, j: (i, j)),
        out_shape=jax.ShapeDtypeStruct((m, D_MODEL), BF16),
        compiler_params=_cparams("parallel", "arbitrary"),
        name="gated_merge",
    )(o_attn, y_s, w_pa, w_ps, rest, rest)


def _layer_norm(x, g, b):
    mu = jnp.mean(x, axis=-1, keepdims=True)
    xc = x - mu
    var = jnp.mean(jnp.square(xc), axis=-1, keepdims=True)
    return xc * lax.rsqrt(var + LN_EPS) * g + b


def _proj_ln_kernel(a_ref, w_ref, res_ref, g_ref, b_ref, o_ref, obf_ref, *, alpha):
    acc = jnp.dot(a_ref[...], w_ref[...], preferred_element_type=F32)
    y = _layer_norm(alpha * res_ref[...] + acc, g_ref[...], b_ref[...])
    o_ref[...] = y
    obf_ref[...] = y.astype(BF16)


def _proj_ln(a, w, res, g, b, *, alpha, tm=256):
    m, k = a.shape
    n = w.shape[1]
    tm = _tile(m, tm)
    return pl.pallas_call(
        functools.partial(_proj_ln_kernel, alpha=alpha),
        grid=(m // tm,),
        in_specs=[pl.BlockSpec((tm, k), lambda i: (i, 0)), pl.BlockSpec((k, n), lambda i: (0, 0)),
                  pl.BlockSpec((tm, n), lambda i: (i, 0)), pl.BlockSpec((1, n), lambda i: (0, 0)),
                  pl.BlockSpec((1, n), lambda i: (0, 0))],
        out_specs=[pl.BlockSpec((tm, n), lambda i: (i, 0)), pl.BlockSpec((tm, n), lambda i: (i, 0))],
        out_shape=[jax.ShapeDtypeStruct((m, n), F32), jax.ShapeDtypeStruct((m, n), BF16)],
        compiler_params=_cparams("parallel"),
        name="out_proj_layernorm",
    )(a, w, res, g, b)


def _gelu(x):
    return 0.5 * x * (1.0 + lax.erf(x * np.float32(np.sqrt(0.5))))


def _glu_long_kernel(h_ref, wg_ref, wu_ref, st_ref, cw_ref, cb_ref, act_ref, tail_ref, ext_scr, *, tm, n_row_tiles):
    i = pl.program_id(1)
    halo = SUBLANES

    @pl.when(i == 0)
    def _init():
        ext_scr[0:halo, :] = st_ref[...]

    sub = _tile(tm, 256)
    for r0 in range(0, tm, sub):
        hh = h_ref[r0:r0 + sub, :]
        g_up = jnp.dot(hh, wg_ref[...], preferred_element_type=F32)
        u = jnp.dot(hh, wu_ref[...], preferred_element_type=F32)
        ext_scr[halo + r0:halo + r0 + sub, :] = g_up
        first = halo - (FFN_CONV - 1) + r0
        acc = ext_scr[first:first + sub, :] * cw_ref[0:1, :]
        for kk in range(1, FFN_CONV):
            acc = acc + ext_scr[first + kk:first + kk + sub, :] * cw_ref[kk:kk + 1, :]
        act_ref[r0:r0 + sub, :] = (_gelu(acc + cb_ref[...]) * u).astype(act_ref.dtype)
    ext_scr[0:halo, :] = ext_scr[tm:tm + halo, :]

    @pl.when(i == n_row_tiles - 1)
    def _tail():
        tail_ref[...] = ext_scr[tm:tm + halo, :]


def _glu_long(h_bf, w_gate, w_up, state, conv_w, conv_b, *, tm=1024, tn=512):
    m, k = h_bf.shape
    tm = _tile(m, tm)
    n_row_tiles = m // tm
    return pl.pallas_call(
        functools.partial(_glu_long_kernel, tm=tm, n_row_tiles=n_row_tiles),
        grid=(D_FF // tn, n_row_tiles),
        in_specs=[pl.BlockSpec((tm, k), lambda j, i: (i, 0)), pl.BlockSpec((k, tn), lambda j, i: (0, j)),
                  pl.BlockSpec((k, tn), lambda j, i: (0, j)), pl.BlockSpec((SUBLANES, tn), lambda j, i: (0, j)),
                  pl.BlockSpec((FFN_CONV, tn), lambda j, i: (0, j)), pl.BlockSpec((1, tn), lambda j, i: (0, j))],
        out_specs=[pl.BlockSpec((tm, tn), lambda j, i: (i, j)), pl.BlockSpec((SUBLANES, tn), lambda j, i: (0, j))],
        out_shape=[jax.ShapeDtypeStruct((m, D_FF), BF16), jax.ShapeDtypeStruct((SUBLANES, D_FF), F32)],
        scratch_shapes=[pltpu.VMEM((SUBLANES + tm, tn), F32)],
        compiler_params=_cparams("parallel", "arbitrary"),
        name="convglu_up_long",
    )(h_bf, w_gate, w_up, state, conv_w, conv_b)


def _glu_short_kernel(h_ref, wg_ref, wu_ref, p1_ref, p2_ref, cw_ref, cb_ref, act_ref, gup_ref, *, seq):
    hh = h_ref[...]
    g_up = jnp.dot(hh, wg_ref[...], preferred_element_type=F32)
    u = jnp.dot(hh, wu_ref[...], preferred_element_type=F32)
    gup_ref[...] = g_up
    r = lax.broadcasted_iota(jnp.int32, g_up.shape, 0) % seq
    back1 = jnp.where(r >= 1, pltpu.roll(g_up, 1, 0), p1_ref[...])
    back2 = jnp.where(r >= 2, pltpu.roll(g_up, 2, 0), p2_ref[...])
    acc = back2 * cw_ref[0:1, :] + back1 * cw_ref[1:2, :] + g_up * cw_ref[2:3, :]
    act_ref[...] = (_gelu(acc + cb_ref[...]) * u).astype(act_ref.dtype)


def _glu_short(h_bf, w_gate, w_up, p1, p2, conv_w, conv_b, *, seq, tn=512):
    m, k = h_bf.shape
    full = lambda j: (0, j)
    return pl.pallas_call(
        functools.partial(_glu_short_kernel, seq=seq),
        grid=(D_FF // tn,),
        in_specs=[pl.BlockSpec((m, k), lambda j: (0, 0)), pl.BlockSpec((k, tn), full), pl.BlockSpec((k, tn), full),
                  pl.BlockSpec((m, tn), full), pl.BlockSpec((m, tn), full),
                  pl.BlockSpec((FFN_CONV, tn), full), pl.BlockSpec((1, tn), full)],
        out_specs=[pl.BlockSpec((m, tn), full), pl.BlockSpec((m, tn), full)],
        out_shape=[jax.ShapeDtypeStruct((m, D_FF), BF16), jax.ShapeDtypeStruct((m, D_FF), F32)],
        compiler_params=_cparams("parallel"),
        name="convglu_up_short",
    )(h_bf, w_gate, w_up, p1, p2, conv_w, conv_b)


def _down_ln_kernel(a_ref, w_ref, res_ref, g_ref, b_ref, o_ref, acc_scr, *, alpha, n_k):
    kk = pl.program_id(1)

    @pl.when(kk == 0)
    def _init():
        acc_scr[...] = jnp.zeros(acc_scr.shape, F32)

    acc_scr[...] += jnp.dot(a_ref[...], w_ref[...], preferred_element_type=F32)

    @pl.when(kk == n_k - 1)
    def _fin():
        o_ref[...] = _layer_norm(alpha * res_ref[...] + acc_scr[...], g_ref[...], b_ref[...])


def _down_ln(a, w, res, g, b, *, alpha, tm=512, tk=D_FF // 4):
    m, k = a.shape
    n = w.shape[1]
    tm = _tile(m, tm)
    n_k = k // tk
    return pl.pallas_call(
        functools.partial(_down_ln_kernel, alpha=alpha, n_k=n_k),
        grid=(m // tm, n_k),
        in_specs=[pl.BlockSpec((tm, tk), lambda i, kk: (i, kk)), pl.BlockSpec((tk, n), lambda i, kk: (kk, 0)),
                  pl.BlockSpec((tm, n), lambda i, kk: (i, 0)), pl.BlockSpec((1, n), lambda i, kk: (0, 0)),
                  pl.BlockSpec((1, n), lambda i, kk: (0, 0))],
        out_specs=pl.BlockSpec((tm, n), lambda i, kk: (i, 0)),
        out_shape=jax.ShapeDtypeStruct((m, n), F32),
        scratch_shapes=[pltpu.VMEM((tm, n), F32)],
        compiler_params=_cparams("parallel", "arbitrary"),
        name="down_proj_layernorm",
    )(a, w, res, g, b)


REST_Z, REST_GA, REST_GS = 0, 1, 2
REST_XBC = 3 * D_MODEL // CONV_DIM
assert REST_XBC * CONV_DIM == 3 * D_MODEL


def _prep_layer(w_in, ssm_conv_w, ssm_conv_b, dt_bias, a_log, d_skip, ssm_norm_g, w_proj_attn, w_proj_ssm, w_out,
                ln1_g, ln1_b, w_gate, w_up, ffn_conv_w, ffn_conv_b, w_down, ln2_g, ln2_b):
    widths = (ATTN_WIDTH, ATTN_WIDTH, ATTN_WIDTH, SSM_INNER, CONV_DIM, SSM_HEADS, D_MODEL, D_MODEL)
    offs = np.concatenate([[0], np.cumsum(widths)])
    wq, wk, wv, wz, wxbc, wdt, wga, wgs = (w_in[:, offs[i]:offs[i + 1]] for i in range(8))
    row = lambda v: v.reshape(1, -1).astype(F32)
    pad_lanes = lambda v: jnp.pad(row(v), ((0, 0), (0, LANES - v.shape[-1])))
    return {
        "w_q": wq.astype(BF16), "w_k": wk.astype(BF16), "w_v": wv.astype(BF16),
        "w_rest": jnp.concatenate([wz, wga, wgs, wxbc], axis=1).astype(BF16),
        "w_dt": jnp.pad(wdt, ((0, 0), (0, LANES - SSM_HEADS))).astype(BF16),
        "conv_w": ssm_conv_w.astype(F32), "conv_b": row(ssm_conv_b),
        "dt_bias": pad_lanes(dt_bias), "a_log": pad_lanes(a_log),
        "d_full": row(jnp.repeat(d_skip, SSM_HEAD_DIM)), "norm_g": row(ssm_norm_g),
        "expand": (jnp.arange(LANES)[:, None] == jnp.arange(SSM_INNER)[None, :] // SSM_HEAD_DIM).astype(F32),
        "w_pa": w_proj_attn.astype(BF16), "w_ps": w_proj_ssm.astype(BF16), "w_out": w_out.astype(BF16),
        "ln1_g": row(ln1_g), "ln1_b": row(ln1_b),
        "w_gate": w_gate.astype(BF16), "w_up": w_up.astype(BF16),
        "ffn_conv_w": ffn_conv_w.astype(F32), "ffn_conv_b": row(ffn_conv_b),
        "w_down": w_down.astype(BF16), "ln2_g": row(ln2_g), "ln2_b": row(ln2_b),
    }


def _rope_tables(pos):
    half = HEAD_DIM // 2
    inv = ROPE_THETA ** (-jnp.arange(half, dtype=F32) * 2.0 / HEAD_DIM)
    ang = pos.astype(F32)[:, None] * inv[None, :]
    cos, sin = jnp.cos(ang), jnp.sin(ang)
    return jnp.concatenate([cos, cos], axis=1), jnp.concatenate([-sin, sin], axis=1)


def _layer(x, pos, p, *, conv_state, ssm_state, ffn_state, paged, alpha):
    bsz, seq, _ = x.shape
    m = bsz * seq
    x2 = x.reshape(m, D_MODEL)
    x_bf = x2.astype(BF16)
    cos, sin = _rope_tables(pos)
    if bsz > 1:
        cos, sin = jnp.tile(cos, (bsz, 1)), jnp.tile(sin, (bsz, 1))
    prompt = paged is None
    q = _project(x_bf, p["w_q"], (cos, sin), name="in_proj_q_rope")
    k_new = _project(x_bf, p["w_k"], (cos, sin), with_bf16=prompt, name="in_proj_k_rope")
    v_new = _project(x_bf, p["w_v"], with_bf16=prompt, name="in_proj_v")
    if prompt:
        (k_new, k_bf), (v_new, v_bf) = k_new, v_new
    rest = _project(x_bf, p["w_rest"], tn=1024, name="in_proj_rest")
    dt_raw = _project(x_bf, p["w_dt"], tn=LANES, name="in_proj_dt")

    xbc = rest[:, REST_XBC * CONV_DIM:(REST_XBC + 1) * CONV_DIM].reshape(bsz, seq, CONV_DIM)
    new_conv = xbc[:, seq - (SSM_CONV - 1):] if seq >= SSM_CONV - 1 else jnp.concatenate(
        [conv_state, xbc], axis=1)[:, -(SSM_CONV - 1):]

    if prompt:
        assert bsz == 1
        o_attn = _attend_prompt(q, k_new, k_bf, v_bf, out_dtype=BF16)
    else:
        cache_k, cache_v, layer, page_table = paged
        o_attn = _attend_paged(cache_k, cache_v, layer, page_table, q, k_new, v_new, dec=seq)

    cst = jnp.pad(conv_state.astype(F32), ((0, 0), (SUBLANES - (SSM_CONV - 1), 0), (0, 0)))
    h0 = ssm_state.astype(F32).reshape(bsz, SSM_GROUPS, SSM_GROUP_WIDTH, SSM_STATE)
    y_s, h_last = _ssd(rest.reshape(bsz, seq, -1), dt_raw.reshape(bsz, seq, LANES), cst, h0, p,
                       xbc_col=REST_XBC, z_col=REST_Z, out_dtype=BF16 if seq % 16 == 0 else F32)
    h_new = h_last.reshape(bsz, SSM_HEADS, SSM_HEAD_DIM, SSM_STATE)

    merged = _merge(o_attn, y_s.reshape(m, SSM_INNER), p["w_pa"], p["w_ps"], rest, ga_col=REST_GA, gs_col=REST_GS)
    h1, h1_bf = _proj_ln(merged, p["w_out"], x2, p["ln1_g"], p["ln1_b"], alpha=alpha)

    if bsz == 1:
        st = jnp.pad(ffn_state[0].astype(F32), ((SUBLANES - (FFN_CONV - 1), 0), (0, 0)))
        act, tail = _glu_long(h1_bf, p["w_gate"], p["w_up"], st, p["ffn_conv_w"], p["ffn_conv_b"])
        assert seq >= FFN_CONV - 1
        new_ffn = tail[None, SUBLANES - (FFN_CONV - 1):]
    else:
        assert seq == SUBLANES
        fs = ffn_state.astype(F32)
        p1 = jnp.pad(fs[:, 1:2], ((0, 0), (0, seq - 1), (0, 0))).reshape(m, D_FF)
        p2 = jnp.pad(fs, ((0, 0), (0, seq - 2), (0, 0))).reshape(m, D_FF)
        act, g_up = _glu_short(h1_bf, p["w_gate"], p["w_up"], p1, p2, p["ffn_conv_w"], p["ffn_conv_b"], seq=seq)
        new_ffn = g_up.reshape(bsz, seq, D_FF)[:, seq - (FFN_CONV - 1):]
    y = _down_ln(act, p["w_down"], h1, p["ln2_g"], p["ln2_b"], alpha=alpha)
    return (y.reshape(bsz, seq, D_MODEL), k_new.reshape(bsz, seq, N_HEADS, HEAD_DIM),
            v_new.reshape(bsz, seq, N_HEADS, HEAD_DIM), h_new, new_conv, new_ffn)


def kernel(x_prompt, x_sample, cache_k, cache_v, state_ssm, state_conv, state_ffn_conv, page_table, w_in, ssm_conv_w, ssm_conv_b, dt_bias, a_log, d_skip, ssm_norm_g, w_proj_attn, w_proj_ssm, w_out, ln1_g, ln1_b, w_gate, w_up, ffn_conv_w, ffn_conv_b, w_down, ln2_g, ln2_b):
    depth = w_in.shape[0]
    alpha = (2.0 * depth) ** 0.25
    bp, lp = x_prompt.shape[0], x_prompt.shape[1]
    past_len = page_table.shape[1] * PAGE_SIZE
    pos_p = jnp.arange(lp, dtype=jnp.int32)
    pos_s = past_len + jnp.arange(x_sample.shape[1], dtype=jnp.int32)
    hp, hs = x_prompt, x_sample
    outs_p, outs_s = [], []
    for l in range(depth):
        p = _prep_layer(w_in[l], ssm_conv_w[l], ssm_conv_b[l], dt_bias[l], a_log[l], d_skip[l], ssm_norm_g[l],
                        w_proj_attn[l], w_proj_ssm[l], w_out[l], ln1_g[l], ln1_b[l], w_gate[l], w_up[l],
                        ffn_conv_w[l], ffn_conv_b[l], w_down[l], ln2_g[l], ln2_b[l])
        conv0 = jnp.zeros((bp, SSM_CONV - 1, CONV_DIM), F32)
        ssm0 = jnp.zeros((bp, SSM_HEADS, SSM_HEAD_DIM, SSM_STATE), F32)
        ffn0 = jnp.zeros((bp, FFN_CONV - 1, D_FF), F32)
        hp, *rest_p = _layer(hp, pos_p, p, conv_state=conv0, ssm_state=ssm0, ffn_state=ffn0, paged=None, alpha=alpha)
        hs, *rest_s = _layer(hs, pos_s, p, conv_state=state_conv[l], ssm_state=state_ssm[l],
                             ffn_state=state_ffn_conv[l], paged=(cache_k, cache_v, l, page_table), alpha=alpha)
        outs_p.append(rest_p)
        outs_s.append(rest_s)
    stack = lambda outs, i: jnp.stack([o[i] for o in outs])
    return (hp, hs, *(stack(outs_p, i) for i in range(5)), *(stack(outs_s, i) for i in range(5)))
```

```python
import functools

import jax
import jax.numpy as jnp
import numpy as np
from jax import lax
from jax.experimental import pallas as pl
from jax.experimental.pallas import tpu as pltpu

F32 = jnp.float32
BF16 = jnp.bfloat16
HIGHEST = lax.Precision.HIGHEST

D_MODEL = 2048
PAGE_SIZE = 128
N_HEADS = 8
HEAD_DIM = 128
ATTN_WIDTH = N_HEADS * HEAD_DIM
MOBA_BLOCK = 256
MOBA_BLOCK_SHIFT = 8
MOBA_TOPK = 3
ROPE_THETA = 10000.0
SSM_INNER = D_MODEL
SSM_HEAD_DIM = 64
SSM_HEADS = SSM_INNER // SSM_HEAD_DIM
SSM_GROUPS = 4
SSM_GROUP_HEADS = SSM_HEADS // SSM_GROUPS
SSM_GROUP_WIDTH = SSM_INNER // SSM_GROUPS
SSM_STATE = 128
SSM_CONV = 4
SSM_CHUNK = 128
CONV_DIM = SSM_INNER + 2 * SSM_GROUPS * SSM_STATE
D_FF = ((8 * D_MODEL // 3 + 255) // 256) * 256
FFN_CONV = 3
LN_EPS = 1e-5
RMS_EPS = 1e-5

LANES = 128
SUBLANES = 8
VMEM_LIMIT_BYTES = 56 * 1024 * 1024

NEG_BIG = -1e30
LOG2_E = 1.4426950408889634


def _cparams(*sem):
    return pltpu.CompilerParams(dimension_semantics=sem, vmem_limit_bytes=VMEM_LIMIT_BYTES)


def _tile(n, pref):
    t = min(pref, n)
    while n % t:
        t //= 2
    return t


def _nt_dot(a, b, precision=None):
    return lax.dot_general(a, b, (((1,), (1,)), ((), ())), precision=precision, preferred_element_type=F32)


def _proj_kernel(*refs, rope, with_bf16):
    a_ref, b_ref = refs[0:2]
    outs = refs[4:] if rope else refs[2:]
    acc = jnp.dot(a_ref[...], b_ref[...], preferred_element_type=F32)
    if rope:
        cos = refs[2][...]
        sin = refs[3][...]
    for h in range(acc.shape[1] // HEAD_DIM):
        cols = slice(h * HEAD_DIM, (h + 1) * HEAD_DIM)
        val = acc[:, cols]
        if rope:
            val = val * cos + pltpu.roll(val, HEAD_DIM // 2, 1) * sin
        outs[0][:, cols] = val
        if with_bf16:
            outs[1][:, cols] = val.astype(BF16)


def _project(a, b, rope=None, *, with_bf16=False, tm=1024, tn=512, name):
    m, k = a.shape
    n = b.shape[1]
    tm, tn = _tile(m, tm), _tile(n, tn)
    assert tn % HEAD_DIM == 0
    in_specs = [pl.BlockSpec((tm, k), lambda i, j: (i, 0)), pl.BlockSpec((k, tn), lambda i, j: (0, j))]
    if rope is not None:
        in_specs += [pl.BlockSpec((tm, HEAD_DIM), lambda i, j: (i, 0))] * 2
    out_spec = pl.BlockSpec((tm, tn), lambda i, j: (i, j))
    out = pl.pallas_call(
        functools.partial(_proj_kernel, rope=rope is not None, with_bf16=with_bf16),
        grid=(m // tm, n // tn),
        in_specs=in_specs,
        out_specs=[out_spec] * (2 if with_bf16 else 1),
        out_shape=[jax.ShapeDtypeStruct((m, n), F32)] + ([jax.ShapeDtypeStruct((m, n), BF16)] if with_bf16 else []),
        compiler_params=_cparams("parallel", "arbitrary"),
        name=name,
    )(a, b, *(rope or ()))
    return out if with_bf16 else out[0]


def _softplus(x):
    return jnp.maximum(x, 0.0) + jnp.log1p(jnp.exp(-jnp.abs(x)))


def _silu(x):
    return x * (0.5 * jnp.tanh(0.5 * x) + 0.5)


def _ssd_kernel(xbc_ref, z_ref, dt_ref, cst_ref, h0_ref, cw_ref, cb_ref, dtb_ref, alog_ref, dfull_ref, ng_ref, expand_ref,
                y_ref, hout_ref,
                ext_scr, xc_scr, dt_scr, ht_scr, yd_scr, yoff_scr, st_scr, *, lc, nchunks):
    q = SSM_CHUNK
    n = SSM_STATE
    c = pl.program_id(1)
    halo = SUBLANES

    @pl.when(c == 0)
    def _init():
        if lc < q:
            ext_scr[...] = jnp.zeros(ext_scr.shape, F32)
            dt_scr[...] = jnp.zeros(dt_scr.shape, F32)
        ext_scr[0:halo, :] = cst_ref[0]
        for g in range(SSM_GROUPS):
            ht_scr[g] = h0_ref[0, g].T

    ext_scr[halo:halo + lc, :] = xbc_ref[0]

    slab = 512
    row = lax.broadcasted_iota(jnp.int32, (q, slab), 0)
    for s in range(CONV_DIM // slab):
        cols = slice(s * slab, (s + 1) * slab)
        acc = ext_scr[halo - 3:halo - 3 + q, cols] * cw_ref[0:1, cols]
        for kk in range(1, SSM_CONV):
            acc = acc + ext_scr[halo - 3 + kk:halo - 3 + kk + q, cols] * cw_ref[kk:kk + 1, cols]
        act = _silu(acc + cb_ref[:, cols])
        if lc < q:
            act = jnp.where(row < lc, act, 0.0)
        xc_scr[:, cols] = act
    if nchunks > 1:
        ext_scr[0:halo, :] = ext_scr[lc:lc + halo, :]

    dt_new = _softplus(dt_ref[0] + dtb_ref[...])
    if lc < q:
        dt_scr[0:lc, :] = dt_new
        dt = dt_scr[...]
    else:
        dt = dt_new
    a = -jnp.exp(alog_ref[...])
    da = dt * a
    r_io = lax.broadcasted_iota(jnp.int32, (q, q), 0)
    c_io = lax.broadcasted_iota(jnp.int32, (q, q), 1)
    causal = r_io >= c_io
    cs = jnp.dot(causal.astype(F32), da, precision=HIGHEST, preferred_element_type=F32)
    cs_t = cs.T
    dt_t = dt.T
    e_exp = jnp.dot(jnp.exp(cs), expand_ref[...], precision=HIGHEST, preferred_element_type=F32)
    w_t = jnp.exp(cs_t[:, q - 1:q] - cs_t) * dt_t

    for g in range(SSM_GROUPS):
        b_g = xc_scr[:, SSM_INNER + g * n:SSM_INNER + (g + 1) * n]
        c_g = xc_scr[:, SSM_INNER + SSM_GROUPS * n + g * n:SSM_INNER + SSM_GROUPS * n + (g + 1) * n]
        c_bf = c_g.astype(BF16)
        cb = _nt_dot(c_bf, b_g.astype(BF16))
        b_t = b_g.T
        yoff_scr[:, g * SSM_GROUP_WIDTH:(g + 1) * SSM_GROUP_WIDTH] = jnp.dot(
            c_bf, ht_scr[g].astype(BF16), preferred_element_type=F32)
        for r in range(SSM_GROUP_HEADS):
            h = g * SSM_GROUP_HEADS + r
            cols = slice(h * SSM_HEAD_DIM, (h + 1) * SSM_HEAD_DIM)
            diff = cs[:, h:h + 1] - cs_t[h:h + 1, :]
            lmat = jnp.exp(jnp.where(causal, diff, -jnp.inf))
            m_h = (cb * lmat * dt_t[h:h + 1, :]).astype(BF16)
            x_h = xc_scr[:, cols].astype(BF16)
            yd_scr[:, cols] = jnp.dot(m_h, x_h, preferred_element_type=F32)
            bts = (b_t * w_t[h:h + 1, :]).astype(BF16)
            st_scr[:, r * SSM_HEAD_DIM:(r + 1) * SSM_HEAD_DIM] = jnp.dot(bts, x_h, preferred_element_type=F32)
        decay = e_exp[q - 1:q, g * SSM_GROUP_WIDTH:(g + 1) * SSM_GROUP_WIDTH]
        ht_scr[g] = ht_scr[g] * decay + st_scr[...]

    xs = xc_scr[0:lc, 0:SSM_INNER]
    y = yd_scr[0:lc, :] + yoff_scr[0:lc, :] * e_exp[0:lc, :] + dfull_ref[...] * xs
    y = y * _silu(z_ref[0])
    for g in range(SSM_GROUPS):
        cols = slice(g * SSM_GROUP_WIDTH, (g + 1) * SSM_GROUP_WIDTH)
        yg = y[:, cols]
        ms = jnp.mean(jnp.square(yg), axis=-1, keepdims=True)
        y_ref[0, :, cols] = (yg * lax.rsqrt(ms + RMS_EPS) * ng_ref[:, cols]).astype(y_ref.dtype)

    @pl.when(c == nchunks - 1)
    def _fin():
        for g in range(SSM_GROUPS):
            hout_ref[0, g] = ht_scr[g].T


def _ssd(src, dt_raw, conv_state, ssm_state, p, *, xbc_col, z_col, out_dtype):
    bsz, seq, _ = src.shape
    lc = min(seq, SSM_CHUNK)
    assert seq % lc == 0 and lc % SUBLANES == 0
    nchunks = seq // lc
    q = SSM_CHUNK
    const = lambda shape: pl.BlockSpec(shape, lambda b, c: (0,) * len(shape))
    y, h_last = pl.pallas_call(
        functools.partial(_ssd_kernel, lc=lc, nchunks=nchunks),
        grid=(bsz, nchunks),
        in_specs=[
            pl.BlockSpec((1, lc, CONV_DIM), lambda b, c: (b, c, xbc_col)),
            pl.BlockSpec((1, lc, SSM_INNER), lambda b, c: (b, c, z_col)),
            pl.BlockSpec((1, lc, LANES), lambda b, c: (b, c, 0)),
            pl.BlockSpec((1, SUBLANES, CONV_DIM), lambda b, c: (b, 0, 0)),
            pl.BlockSpec((1, SSM_GROUPS, SSM_GROUP_WIDTH, SSM_STATE), lambda b, c: (b, 0, 0, 0)),
            const((SSM_CONV, CONV_DIM)), const((1, CONV_DIM)), const((1, LANES)), const((1, LANES)),
            const((1, SSM_INNER)), const((1, SSM_INNER)), const((LANES, SSM_INNER)),
        ],
        out_specs=[
            pl.BlockSpec((1, lc, SSM_INNER), lambda b, c: (b, c, 0)),
            pl.BlockSpec((1, SSM_GROUPS, SSM_GROUP_WIDTH, SSM_STATE), lambda b, c: (b, 0, 0, 0)),
        ],
        out_shape=[jax.ShapeDtypeStruct((bsz, seq, SSM_INNER), out_dtype),
                   jax.ShapeDtypeStruct((bsz, SSM_GROUPS, SSM_GROUP_WIDTH, SSM_STATE), F32)],
        scratch_shapes=[
            pltpu.VMEM((SUBLANES + q, CONV_DIM), F32),
            pltpu.VMEM((q, CONV_DIM), F32),
            pltpu.VMEM((q, LANES), F32),
            pltpu.VMEM((SSM_GROUPS, SSM_STATE, SSM_GROUP_WIDTH), F32),
            pltpu.VMEM((q, SSM_INNER), F32),
            pltpu.VMEM((q, SSM_INNER), F32),
            pltpu.VMEM((SSM_STATE, SSM_GROUP_WIDTH), F32),
        ],
        compiler_params=_cparams("parallel", "arbitrary"),
        name="ssd",
    )(src, src, dt_raw, conv_state, ssm_state, p["conv_w"], p["conv_b"], p["dt_bias"], p["a_log"], p["d_full"],
      p["norm_g"], p["expand"])
    return y, h_last


def _select_blocks(gate, own, n_sel):
    blk = lax.broadcasted_iota(jnp.int32, gate.shape, 1)
    blk_f = blk.astype(F32)
    g = jnp.where(blk < own, gate, -jnp.inf)
    sel = jnp.zeros(gate.shape, F32)
    for t in range(n_sel):
        m = jnp.max(g, axis=1, keepdims=True)
        idx = jnp.min(jnp.where(g == m, blk_f, float(LANES)), axis=1, keepdims=True)
        hit = blk_f == idx
        sel = jnp.where(hit & (own > t), 1.0, sel)
        g = jnp.where(hit, -jnp.inf, g)
    return sel


def _kmean_kernel(k_ref, o_ref):
    rows = k_ref.shape[0]
    o_ref[...] = jnp.sum(k_ref[...].reshape(rows // MOBA_BLOCK, MOBA_BLOCK, k_ref.shape[1]), axis=1) / MOBA_BLOCK


def _block_means(k, *, nblk_pad):
    t = k.shape[0]
    nblk = t // MOBA_BLOCK
    per = _tile(nblk, SUBLANES)
    assert per == SUBLANES or per == nblk
    out = pl.pallas_call(
        _kmean_kernel,
        grid=(nblk // per,),
        in_specs=[pl.BlockSpec((per * MOBA_BLOCK, ATTN_WIDTH), lambda i: (i, 0))],
        out_specs=pl.BlockSpec((per, ATTN_WIDTH), lambda i: (i, 0)),
        out_shape=jax.ShapeDtypeStruct((nblk, ATTN_WIDTH), F32),
        compiler_params=_cparams("parallel"),
        name="moba_block_means",
    )(k)
    return jnp.pad(out, ((0, nblk_pad - nblk), (0, 0)))


def _attn_kernel(qi_ref, kj_ref, qf_ref, k_ref, koh_ref, v_ref, km_ref, o_ref,
                 qa_scr, m_scr, acc_scr, *, tile, sub, n_sel, scale):
    p = pl.program_id(1)
    qi = qi_ref[p]
    kj = kj_ref[p]
    q_pos = qi * tile + lax.broadcasted_iota(jnp.int32, (tile, 1), 0)
    own = lax.shift_right_logical(q_pos, MOBA_BLOCK_SHIFT)

    @pl.when(kj == 0)
    def _init():
        m_scr[...] = jnp.full(m_scr.shape, NEG_BIG, F32)
        acc_scr[...] = jnp.zeros(acc_scr.shape, F32)
        qf = qf_ref[...]
        gate = _nt_dot(qf, km_ref[...], precision=HIGHEST)
        sel = _select_blocks(gate, own, n_sel)
        blk = lax.broadcasted_iota(jnp.int32, sel.shape, 1)
        qa_scr[:, 0:HEAD_DIM] = (qf * (scale * LOG2_E)).astype(BF16)
        qa_scr[:, HEAD_DIM:] = jnp.where((sel > 0.5) | (blk == own), 0.0, NEG_BIG).astype(BF16)

    k_aug = jnp.concatenate([k_ref[...], koh_ref[...]], axis=1)
    v_aug = jnp.concatenate([v_ref[...], jnp.ones((tile, HEAD_DIM), BF16)], axis=1)

    def update(r0, n_keys, diag):
        rows = slice(r0, r0 + sub)
        s = _nt_dot(qa_scr[rows, :], k_aug[0:n_keys])
        if diag:
            k_pos = kj * tile + lax.broadcasted_iota(jnp.int32, (1, n_keys), 1)
            future = (lax.shift_right_logical(k_pos, MOBA_BLOCK_SHIFT) == own[rows]) & (k_pos > q_pos[rows])
            s = jnp.where(future, NEG_BIG, s)
        m_old = m_scr[rows, :]
        m_new = jnp.maximum(m_old, jnp.max(s, axis=1, keepdims=True))
        alpha = jnp.exp2(m_old - m_new)
        pr = jnp.exp2(s - m_new).astype(BF16)
        acc_scr[rows, :] = alpha * acc_scr[rows, :] + jnp.dot(pr, v_aug[0:n_keys], preferred_element_type=F32)
        m_scr[rows, :] = m_new

    @pl.when(kj < qi)
    def _past():
        for r0 in range(0, tile, sub):
            update(r0, tile, False)

    @pl.when(kj == qi)
    def _diag():
        for r0 in range(0, tile, sub):
            update(r0, r0 + sub, True)
        o_ref[...] = (acc_scr[:, 0:HEAD_DIM] / acc_scr[:, HEAD_DIM:]).astype(o_ref.dtype)


def _attend_prompt(q, k, k_bf, v_bf, *, out_dtype):
    t = q.shape[0]
    assert t % MOBA_BLOCK == 0
    nblk = t // MOBA_BLOCK
    assert nblk <= LANES
    n_sel = min(MOBA_TOPK, nblk)
    kmean = _block_means(k, nblk_pad=LANES)
    tile = _tile(t, 1024)
    assert tile % MOBA_BLOCK == 0
    pairs = [(i, j) for i in range(t // tile) for j in range(i + 1)]
    qi_arr = jnp.asarray([a for a, _ in pairs], jnp.int32)
    kj_arr = jnp.asarray([b for _, b in pairs], jnp.int32)
    k_onehot = (jnp.arange(t)[:, None] // MOBA_BLOCK == jnp.arange(LANES)[None, :]).astype(BF16)
    grid_spec = pltpu.PrefetchScalarGridSpec(
        num_scalar_prefetch=2,
        grid=(N_HEADS, len(pairs)),
        in_specs=[
            pl.BlockSpec((tile, HEAD_DIM), lambda h, p, qi, kj: (qi[p], h)),
            pl.BlockSpec((tile, HEAD_DIM), lambda h, p, qi, kj: (kj[p], h)),
            pl.BlockSpec((tile, LANES), lambda h, p, qi, kj: (kj[p], 0)),
            pl.BlockSpec((tile, HEAD_DIM), lambda h, p, qi, kj: (kj[p], h)),
            pl.BlockSpec((LANES, HEAD_DIM), lambda h, p, qi, kj: (0, h)),
        ],
        out_specs=pl.BlockSpec((tile, HEAD_DIM), lambda h, p, qi, kj: (qi[p], h)),
        scratch_shapes=[pltpu.VMEM((tile, 2 * HEAD_DIM), BF16),
                        pltpu.VMEM((tile, 1), F32),
                        pltpu.VMEM((tile, 2 * HEAD_DIM), F32)],
    )
    return pl.pallas_call(
        functools.partial(_attn_kernel, tile=tile, sub=MOBA_BLOCK, n_sel=n_sel, scale=HEAD_DIM ** -0.5),
        grid_spec=grid_spec,
        out_shape=jax.ShapeDtypeStruct((t, ATTN_WIDTH), out_dtype),
        compiler_params=_cparams("parallel", "arbitrary"),
        name="moba_prompt_attention",
    )(qi_arr, kj_arr, q, k_bf, k_onehot, v_bf, kmean)


def _page_head(page_ref, h):
    return page_ref[0, pl.ds(h, PAGE_SIZE, stride=N_HEADS), :]


def _paged_attn_kernel(pt_ref, *refs, gp, n_steps, dec, past, n_sel, scale):
    k_refs = refs[0:gp]
    v_refs = refs[gp:2 * gp]
    q_ref, kn_ref, vn_ref, o_ref = refs[2 * gp:2 * gp + 4]
    qbd_scr, kbf_scr, s_scr, ksum_scr, sel_scr, l_scr, oacc_scr, new_scr = refs[2 * gp + 4:]
    ph = pl.program_id(1)
    st = pl.program_id(2)
    page = PAGE_SIZE
    step_keys = gp * page
    t_past = past
    cols = N_HEADS * dec
    own = past // MOBA_BLOCK

    @pl.when((ph == 0) & (st == 0))
    def _init():
        new_scr[...] = jnp.zeros(new_scr.shape, F32)
        new_scr[0:dec, :] = q_ref[...]
        for h in range(N_HEADS):
            qt = new_scr[:, h * HEAD_DIM:(h + 1) * HEAD_DIM].T
            if h:
                qt = pltpu.roll(qt, h * dec, 1)
            qbd_scr[h * HEAD_DIM:(h + 1) * HEAD_DIM, :] = qt.astype(BF16)
        ksum_scr[...] = jnp.zeros(ksum_scr.shape, F32)

    @pl.when(ph == 0)
    def _scores():
        pages_per_blk = MOBA_BLOCK // page
        blks_per_step = gp // pages_per_blk
        sums = []
        for g in range(0, gp, pages_per_blk):
            for h in range(N_HEADS):
                tot = None
                for gg in range(g, g + pages_per_blk):
                    kh = _page_head(k_refs[gg], h)
                    kbf_scr[gg * page:(gg + 1) * page, h * HEAD_DIM:(h + 1) * HEAD_DIM] = kh.astype(BF16)
                    part = jnp.sum(kh, axis=0, keepdims=True)
                    tot = part if tot is None else tot + part
                sums.append(tot)
            s_blk = jnp.dot(kbf_scr[g * page:(g + pages_per_blk) * page, :], qbd_scr[...],
                            preferred_element_type=F32) * scale
            s_scr[pl.ds(pl.multiple_of(st * step_keys + g * page, MOBA_BLOCK), MOBA_BLOCK), :] = s_blk
        for s_static in range(n_steps):
            @pl.when(st == s_static)
            def _store_sums(s_static=s_static):
                for j, tot in enumerate(sums):
                    blk = s_static * blks_per_step + j // N_HEADS
                    h = j % N_HEADS
                    ksum_scr[blk:blk + 1, h * HEAD_DIM:(h + 1) * HEAD_DIM] = tot

    @pl.when((ph == 0) & (st == n_steps - 1))
    def _softmax():
        new_scr[...] = jnp.zeros(new_scr.shape, F32)
        new_scr[0:dec, :] = kn_ref[...]
        s_scr[t_past:t_past + page, :] = jnp.dot(new_scr[...].astype(BF16), qbd_scr[...],
                                                 preferred_element_type=F32) * scale
        own_col = jnp.full((dec, 1), own, jnp.int32)
        sel_scr[...] = jnp.zeros(sel_scr.shape, F32)
        for h in range(N_HEADS):
            kmean = ksum_scr[:, h * HEAD_DIM:(h + 1) * HEAD_DIM] / MOBA_BLOCK
            gate = _nt_dot(q_ref[:, h * HEAD_DIM:(h + 1) * HEAD_DIM], kmean, precision=HIGHEST)
            sel_scr[h * dec:(h + 1) * dec, :] = _select_blocks(gate, own_col, n_sel)
        sel_bf = sel_scr[...].astype(BF16)
        col = lax.broadcasted_iota(jnp.int32, (page, LANES), 1)
        krow = lax.broadcasted_iota(jnp.int32, (page, LANES), 0)
        new_ok = (krow <= col % dec) & (col < cols)
        s_new = jnp.where(new_ok, s_scr[t_past:t_past + page, :], NEG_BIG)
        s_scr[t_past:t_past + page, :] = s_new
        blk_col = lax.broadcasted_iota(jnp.int32, (step_keys, LANES), 1)
        key_row = lax.broadcasted_iota(jnp.int32, (step_keys, LANES), 0)

        def mask_body(ci, m):
            rows = pl.ds(pl.multiple_of(ci * step_keys, step_keys), step_keys)
            onehot = (lax.shift_right_logical(ci * step_keys + key_row, MOBA_BLOCK_SHIFT) == blk_col).astype(BF16)
            picked = _nt_dot(onehot, sel_bf)
            sc = jnp.where(picked > 0.5, s_scr[rows, :], NEG_BIG)
            s_scr[rows, :] = sc
            return jnp.maximum(m, jnp.max(sc, axis=0, keepdims=True))

        m = lax.fori_loop(0, n_steps, mask_body, jnp.max(s_new, axis=0, keepdims=True))

        def exp_body(ci, carry):
            rows = pl.ds(pl.multiple_of(ci * step_keys, step_keys), step_keys)
            s_scr[rows, :] = jnp.exp(s_scr[rows, :] - m)
            return carry

        lax.fori_loop(0, n_steps, exp_body, 0)
        s_scr[t_past:t_past + page, :] = jnp.exp(s_new - m)
        l_scr[...] = jnp.zeros(l_scr.shape, F32)
        oacc_scr[...] = jnp.zeros(oacc_scr.shape, F32)

    @pl.when(ph == 1)
    def _values():
        pages_per_blk = MOBA_BLOCK // page
        l_tot = None
        o_tot = None
        for g in range(0, gp, pages_per_blk):
            for gg in range(g, g + pages_per_blk):
                for h in range(N_HEADS):
                    kbf_scr[gg * page:(gg + 1) * page, h * HEAD_DIM:(h + 1) * HEAD_DIM] = _page_head(
                        v_refs[gg], h).astype(BF16)
            rows = pl.ds(pl.multiple_of(st * step_keys + g * page, MOBA_BLOCK), MOBA_BLOCK)
            pr = s_scr[rows, :].T
            l_blk = jnp.sum(pr, axis=1, keepdims=True)
            o_blk = jnp.dot(pr.astype(BF16), kbf_scr[g * page:(g + pages_per_blk) * page, :],
                            preferred_element_type=F32)
            l_tot = l_blk if l_tot is None else l_tot + l_blk
            o_tot = o_blk if o_tot is None else o_tot + o_blk
        l_scr[...] += l_tot
        oacc_scr[...] += o_tot

    @pl.when((ph == 1) & (st == n_steps - 1))
    def _fin():
        new_scr[...] = jnp.zeros(new_scr.shape, F32)
        new_scr[0:dec, :] = vn_ref[...]
        pr = s_scr[t_past:t_past + page, :].T
        l_tot = l_scr[...] + jnp.sum(pr, axis=1, keepdims=True)
        o_full = oacc_scr[...] + jnp.dot(pr.astype(BF16), new_scr[...].astype(BF16), preferred_element_type=F32)
        for h in range(N_HEADS):
            o_ref[:, h * HEAD_DIM:(h + 1) * HEAD_DIM] = (
                o_full[h * dec:(h + 1) * dec, h * HEAD_DIM:(h + 1) * HEAD_DIM] / l_tot[h * dec:(h + 1) * dec, :]
            ).astype(o_ref.dtype)


def _attend_paged(cache_k, cache_v, layer, page_table, q, k_new, v_new, *, dec):
    bsz, n_pages = page_table.shape
    past = n_pages * PAGE_SIZE
    assert past % MOBA_BLOCK == 0 and dec == SUBLANES and N_HEADS * dec <= LANES
    own = past // MOBA_BLOCK
    assert own + 1 <= LANES
    n_sel = min(MOBA_TOPK, own + 1)
    depth, n_pool = cache_k.shape[0], cache_k.shape[1]
    ck = cache_k.reshape(depth * n_pool, PAGE_SIZE * N_HEADS, HEAD_DIM)
    cv = cache_v.reshape(depth * n_pool, PAGE_SIZE * N_HEADS, HEAD_DIM)
    first = layer * n_pool
    gp = _tile(n_pages, 16)
    assert gp % (MOBA_BLOCK // PAGE_SIZE) == 0
    n_steps = n_pages // gp
    t_all = past + PAGE_SIZE

    def k_map(g):
        return lambda b, ph, st, pt: (first + pt[b, jnp.where(ph == 0, st, n_steps - 1) * gp + g], 0, 0)

    def v_map(g):
        return lambda b, ph, st, pt: (first + pt[b, jnp.where(ph == 1, st, 0) * gp + g], 0, 0)

    page_spec = lambda fn: pl.BlockSpec((1, PAGE_SIZE * N_HEADS, HEAD_DIM), fn)
    row_spec = pl.BlockSpec((dec, ATTN_WIDTH), lambda b, ph, st, pt: (b, 0))
    grid_spec = pltpu.PrefetchScalarGridSpec(
        num_scalar_prefetch=1,
        grid=(bsz, 2, n_steps),
        in_specs=[page_spec(k_map(g)) for g in range(gp)] + [page_spec(v_map(g)) for g in range(gp)]
        + [row_spec] * 3,
        out_specs=row_spec,
        scratch_shapes=[
            pltpu.VMEM((ATTN_WIDTH, LANES), BF16),
            pltpu.VMEM((gp * PAGE_SIZE, ATTN_WIDTH), BF16),
            pltpu.VMEM((t_all, LANES), F32),
            pltpu.VMEM((LANES, ATTN_WIDTH), F32),
            pltpu.VMEM((LANES, LANES), F32),
            pltpu.VMEM((LANES, 1), F32),
            pltpu.VMEM((LANES, ATTN_WIDTH), F32),
            pltpu.VMEM((PAGE_SIZE, ATTN_WIDTH), F32),
        ],
    )
    return pl.pallas_call(
        functools.partial(_paged_attn_kernel, gp=gp, n_steps=n_steps, dec=dec, past=past, n_sel=n_sel,
                          scale=HEAD_DIM ** -0.5),
        grid_spec=grid_spec,
        out_shape=jax.ShapeDtypeStruct((bsz * dec, ATTN_WIDTH), F32),
        compiler_params=_cparams("parallel", "arbitrary", "arbitrary"),
        name="moba_paged_attention",
    )(page_table, *([ck] * gp), *([cv] * gp), q, k_new, v_new)


def _merge_kernel(oa_ref, ys_ref, wa_ref, ws_ref, ga_ref, gs_ref, o_ref):
    pa = jnp.dot(oa_ref[...].astype(BF16), wa_ref[...], preferred_element_type=F32)
    ps = jnp.dot(ys_ref[...].astype(BF16), ws_ref[...], preferred_element_type=F32)
    o_ref[...] = (jax.nn.sigmoid(ga_ref[...]) * pa + jax.nn.sigmoid(gs_ref[...]) * ps).astype(o_ref.dtype)


def _merge(o_attn, y_s, w_pa, w_ps, rest, *, ga_col, gs_col, tm=512, tn=512):
    m = o_attn.shape[0]
    tm = _tile(m, tm)
    return pl.pallas_call(
        _merge_kernel,
        grid=(m // tm, D_MODEL // tn),
        in_specs=[
            pl.BlockSpec((tm, ATTN_WIDTH), lambda i, j: (i, 0)),
            pl.BlockSpec((tm, SSM_INNER), lambda i, j: (i, 0)),
            pl.BlockSpec((ATTN_WIDTH, tn), lambda i, j: (0, j)),
            pl.BlockSpec((SSM_INNER, tn), lambda i, j: (0, j)),
            pl.BlockSpec((tm, tn), lambda i, j: (i, ga_col * (D_MODEL // tn) + j)),
            pl.BlockSpec((tm, tn), lambda i, j: (i, gs_col * (D_MODEL // tn) + j)),
        ],
        out_specs=pl.BlockSpec((tm, tn), lambda i, j: (i, j)),
        out_shape=jax.ShapeDtypeStruct((m, D_MODEL), BF16),
        compiler_params=_cparams("parallel", "arbitrary"),
        name="gated_merge",
    )(o_attn, y_s, w_pa, w_ps, rest, rest)


def _layer_norm(x, g, b):
    mu = jnp.mean(x, axis=-1, keepdims=True)
    xc = x - mu
    var = jnp.mean(jnp.square(xc), axis=-1, keepdims=True)
    return xc * lax.rsqrt(var + LN_EPS) * g + b


def _proj_ln_kernel(a_ref, w_ref, res_ref, g_ref, b_ref, o_ref, obf_ref, *, alpha):
    acc = jnp.dot(a_ref[...], w_ref[...], preferred_element_type=F32)
    y = _layer_norm(alpha * res_ref[...] + acc, g_ref[...], b_ref[...])
    o_ref[...] = y
    obf_ref[...] = y.astype(BF16)


def _proj_ln(a, w, res, g, b, *, alpha, tm=256):
    m, k = a.shape
    n = w.shape[1]
    tm = _tile(m, tm)
    return pl.pallas_call(
        functools.partial(_proj_ln_kernel, alpha=alpha),
        grid=(m // tm,),
        in_specs=[pl.BlockSpec((tm, k), lambda i: (i, 0)), pl.BlockSpec((k, n), lambda i: (0, 0)),
                  pl.BlockSpec((tm, n), lambda i: (i, 0)), pl.BlockSpec((1, n), lambda i: (0, 0)),
                  pl.BlockSpec((1, n), lambda i: (0, 0))],
        out_specs=[pl.BlockSpec((tm, n), lambda i: (i, 0)), pl.BlockSpec((tm, n), lambda i: (i, 0))],
        out_shape=[jax.ShapeDtypeStruct((m, n), F32), jax.ShapeDtypeStruct((m, n), BF16)],
        compiler_params=_cparams("parallel"),
        name="out_proj_layernorm",
    )(a, w, res, g, b)


def _gelu(x):
    return 0.5 * x * (1.0 + lax.erf(x * np.float32(np.sqrt(0.5))))


def _glu_long_kernel(h_ref, wg_ref, wu_ref, st_ref, cw_ref, cb_ref, act_ref, tail_ref, ext_scr, *, tm, n_row_tiles):
    i = pl.program_id(1)
    halo = SUBLANES

    @pl.when(i == 0)
    def _init():
        ext_scr[0:halo, :] = st_ref[...]

    sub = _tile(tm, 256)
    for r0 in range(0, tm, sub):
        hh = h_ref[r0:r0 + sub, :]
        g_up = jnp.dot(hh, wg_ref[...], preferred_element_type=F32)
        u = jnp.dot(hh, wu_ref[...], preferred_element_type=F32)
        ext_scr[halo + r0:halo + r0 + sub, :] = g_up
        first = halo - (FFN_CONV - 1) + r0
        acc = ext_scr[first:first + sub, :] * cw_ref[0:1, :]
        for kk in range(1, FFN_CONV):
            acc = acc + ext_scr[first + kk:first + kk + sub, :] * cw_ref[kk:kk + 1, :]
        act_ref[r0:r0 + sub, :] = (_gelu(acc + cb_ref[...]) * u).astype(act_ref.dtype)
    ext_scr[0:halo, :] = ext_scr[tm:tm + halo, :]

    @pl.when(i == n_row_tiles - 1)
    def _tail():
        tail_ref[...] = ext_scr[tm:tm + halo, :]


def _glu_long(h_bf, w_gate, w_up, state, conv_w, conv_b, *, tm=1024, tn=512):
    m, k = h_bf.shape
    tm = _tile(m, tm)
    n_row_tiles = m // tm
    return pl.pallas_call(
        functools.partial(_glu_long_kernel, tm=tm, n_row_tiles=n_row_tiles),
        grid=(D_FF // tn, n_row_tiles),
        in_specs=[pl.BlockSpec((tm, k), lambda j, i: (i, 0)), pl.BlockSpec((k, tn), lambda j, i: (0, j)),
                  pl.BlockSpec((k, tn), lambda j, i: (0, j)), pl.BlockSpec((SUBLANES, tn), lambda j, i: (0, j)),
                  pl.BlockSpec((FFN_CONV, tn), lambda j, i: (0, j)), pl.BlockSpec((1, tn), lambda j, i: (0, j))],
        out_specs=[pl.BlockSpec((tm, tn), lambda j, i: (i, j)), pl.BlockSpec((SUBLANES, tn), lambda j, i: (0, j))],
        out_shape=[jax.ShapeDtypeStruct((m, D_FF), BF16), jax.ShapeDtypeStruct((SUBLANES, D_FF), F32)],
        scratch_shapes=[pltpu.VMEM((SUBLANES + tm, tn), F32)],
        compiler_params=_cparams("parallel", "arbitrary"),
        name="convglu_up_long",
    )(h_bf, w_gate, w_up, state, conv_w, conv_b)


def _glu_short_kernel(h_ref, wg_ref, wu_ref, p1_ref, p2_ref, cw_ref, cb_ref, act_ref, gup_ref, *, seq):
    hh = h_ref[...]
    g_up = jnp.dot(hh, wg_ref[...], preferred_element_type=F32)
    u = jnp.dot(hh, wu_ref[...], preferred_element_type=F32)
    gup_ref[...] = g_up
    r = lax.broadcasted_iota(jnp.int32, g_up.shape, 0) % seq
    back1 = jnp.where(r >= 1, pltpu.roll(g_up, 1, 0), p1_ref[...])
    back2 = jnp.where(r >= 2, pltpu.roll(g_up, 2, 0), p2_ref[...])
    acc = back2 * cw_ref[0:1, :] + back1 * cw_ref[1:2, :] + g_up * cw_ref[2:3, :]
    act_ref[...] = (_gelu(acc + cb_ref[...]) * u).astype(act_ref.dtype)


def _glu_short(h_bf, w_gate, w_up, p1, p2, conv_w, conv_b, *, seq, tn=512):
    m, k = h_bf.shape
    full = lambda j: (0, j)
    return pl.pallas_call(
        functools.partial(_glu_short_kernel, seq=seq),
        grid=(D_FF // tn,),
        in_specs=[pl.BlockSpec((m, k), lambda j: (0, 0)), pl.BlockSpec((k, tn), full), pl.BlockSpec((k, tn), full),
                  pl.BlockSpec((m, tn), full), pl.BlockSpec((m, tn), full),
                  pl.BlockSpec((FFN_CONV, tn), full), pl.BlockSpec((1, tn), full)],
        out_specs=[pl.BlockSpec((m, tn), full), pl.BlockSpec((m, tn), full)],
        out_shape=[jax.ShapeDtypeStruct((m, D_FF), BF16), jax.ShapeDtypeStruct((m, D_FF), F32)],
        compiler_params=_cparams("parallel"),
        name="convglu_up_short",
    )(h_bf, w_gate, w_up, p1, p2, conv_w, conv_b)


def _down_ln_kernel(a_ref, w_ref, res_ref, g_ref, b_ref, o_ref, acc_scr, *, alpha, n_k):
    kk = pl.program_id(1)

    @pl.when(kk == 0)
    def _init():
        acc_scr[...] = jnp.zeros(acc_scr.shape, F32)

    acc_scr[...] += jnp.dot(a_ref[...], w_ref[...], preferred_element_type=F32)

    @pl.when(kk == n_k - 1)
    def _fin():
        o_ref[...] = _layer_norm(alpha * res_ref[...] + acc_scr[...], g_ref[...], b_ref[...])


def _down_ln(a, w, res, g, b, *, alpha, tm=512, tk=D_FF // 4):
    m, k = a.shape
    n = w.shape[1]
    tm = _tile(m, tm)
    n_k = k // tk
    return pl.pallas_call(
        functools.partial(_down_ln_kernel, alpha=alpha, n_k=n_k),
        grid=(m // tm, n_k),
        in_specs=[pl.BlockSpec((tm, tk), lambda i, kk: (i, kk)), pl.BlockSpec((tk, n), lambda i, kk: (kk, 0)),
                  pl.BlockSpec((tm, n), lambda i, kk: (i, 0)), pl.BlockSpec((1, n), lambda i, kk: (0, 0)),
                  pl.BlockSpec((1, n), lambda i, kk: (0, 0))],
        out_specs=pl.BlockSpec((tm, n), lambda i, kk: (i, 0)),
        out_shape=jax.ShapeDtypeStruct((m, n), F32),
        scratch_shapes=[pltpu.VMEM((tm, n), F32)],
        compiler_params=_cparams("parallel", "arbitrary"),
        name="down_proj_layernorm",
    )(a, w, res, g, b)


REST_Z, REST_GA, REST_GS = 0, 1, 2
REST_XBC = 3 * D_MODEL // CONV_DIM
assert REST_XBC * CONV_DIM == 3 * D_MODEL


def _prep_layer(w_in, ssm_conv_w, ssm_conv_b, dt_bias, a_log, d_skip, ssm_norm_g, w_proj_attn, w_proj_ssm, w_out,
                ln1_g, ln1_b, w_gate, w_up, ffn_conv_w, ffn_conv_b, w_down, ln2_g, ln2_b):
    widths = (ATTN_WIDTH, ATTN_WIDTH, ATTN_WIDTH, SSM_INNER, CONV_DIM, SSM_HEADS, D_MODEL, D_MODEL)
    offs = np.concatenate([[0], np.cumsum(widths)])
    wq, wk, wv, wz, wxbc, wdt, wga, wgs = (w_in[:, offs[i]:offs[i + 1]] for i in range(8))
    row = lambda v: v.reshape(1, -1).astype(F32)
    pad_lanes = lambda v: jnp.pad(row(v), ((0, 0), (0, LANES - v.shape[-1])))
    return {
        "w_q": wq.astype(BF16), "w_k": wk.astype(BF16), "w_v": wv.astype(BF16),
        "w_rest": jnp.concatenate([wz, wga, wgs, wxbc], axis=1).astype(BF16),
        "w_dt": jnp.pad(wdt, ((0, 0), (0, LANES - SSM_HEADS))).astype(BF16),
        "conv_w": ssm_conv_w.astype(F32), "conv_b": row(ssm_conv_b),
        "dt_bias": pad_lanes(dt_bias), "a_log": pad_lanes(a_log),
        "d_full": row(jnp.repeat(d_skip, SSM_HEAD_DIM)), "norm_g": row(ssm_norm_g),
        "expand": (jnp.arange(LANES)[:, None] == jnp.arange(SSM_INNER)[None, :] // SSM_HEAD_DIM).astype(F32),
        "w_pa": w_proj_attn.astype(BF16), "w_ps": w_proj_ssm.astype(BF16), "w_out": w_out.astype(BF16),
        "ln1_g": row(ln1_g), "ln1_b": row(ln1_b),
        "w_gate": w_gate.astype(BF16), "w_up": w_up.astype(BF16),
        "ffn_conv_w": ffn_conv_w.astype(F32), "ffn_conv_b": row(ffn_conv_b),
        "w_down": w_down.astype(BF16), "ln2_g": row(ln2_g), "ln2_b": row(ln2_b),
    }


def _rope_tables(pos):
    half = HEAD_DIM // 2
    inv = ROPE_THETA ** (-jnp.arange(half, dtype=F32) * 2.0 / HEAD_DIM)
    ang = pos.astype(F32)[:, None] * inv[None, :]
    cos, sin = jnp.cos(ang), jnp.sin(ang)
    return jnp.concatenate([cos, cos], axis=1), jnp.concatenate([-sin, sin], axis=1)


def _layer(x, pos, p, *, conv_state, ssm_state, ffn_state, paged, alpha):
    bsz, seq, _ = x.shape
    m = bsz * seq
    x2 = x.reshape(m, D_MODEL)
    x_bf = x2.astype(BF16)
    cos, sin = _rope_tables(pos)
    if bsz > 1:
        cos, sin = jnp.tile(cos, (bsz, 1)), jnp.tile(sin, (bsz, 1))
    prompt = paged is None
    q = _project(x_bf, p["w_q"], (cos, sin), name="in_proj_q_rope")
    k_new = _project(x_bf, p["w_k"], (cos, sin), with_bf16=prompt, name="in_proj_k_rope")
    v_new = _project(x_bf, p["w_v"], with_bf16=prompt, name="in_proj_v")
    if prompt:
        (k_new, k_bf), (v_new, v_bf) = k_new, v_new
    rest = _project(x_bf, p["w_rest"], tn=1024, name="in_proj_rest")
    dt_raw = _project(x_bf, p["w_dt"], tn=LANES, name="in_proj_dt")

    assert seq >= SSM_CONV - 1
    new_conv = rest.reshape(bsz, seq, -1)[:, seq - (SSM_CONV - 1):, REST_XBC * CONV_DIM:(REST_XBC + 1) * CONV_DIM]

    if prompt:
        assert bsz == 1
        o_attn = _attend_prompt(q, k_new, k_bf, v_bf, out_dtype=BF16)
    else:
        cache_k, cache_v, layer, page_table = paged
        o_attn = _attend_paged(cache_k, cache_v, layer, page_table, q, k_new, v_new, dec=seq)

    cst = jnp.pad(conv_state.astype(F32), ((0, 0), (SUBLANES - (SSM_CONV - 1), 0), (0, 0)))
    h0 = ssm_state.astype(F32).reshape(bsz, SSM_GROUPS, SSM_GROUP_WIDTH, SSM_STATE)
    y_s, h_last = _ssd(rest.reshape(bsz, seq, -1), dt_raw.reshape(bsz, seq, LANES), cst, h0, p,
                       xbc_col=REST_XBC, z_col=REST_Z, out_dtype=BF16 if seq % 16 == 0 else F32)
    h_new = h_last.reshape(bsz, SSM_HEADS, SSM_HEAD_DIM, SSM_STATE)

    merged = _merge(o_attn, y_s.reshape(m, SSM_INNER), p["w_pa"], p["w_ps"], rest, ga_col=REST_GA, gs_col=REST_GS)
    h1, h1_bf = _proj_ln(merged, p["w_out"], x2, p["ln1_g"], p["ln1_b"], alpha=alpha)

    if bsz == 1:
        st = jnp.pad(ffn_state[0].astype(F32), ((SUBLANES - (FFN_CONV - 1), 0), (0, 0)))
        act, tail = _glu_long(h1_bf, p["w_gate"], p["w_up"], st, p["ffn_conv_w"], p["ffn_conv_b"])
        assert seq >= FFN_CONV - 1
        new_ffn = tail[None, SUBLANES - (FFN_CONV - 1):]
    else:
        assert seq == SUBLANES
        fs = ffn_state.astype(F32)
        p1 = jnp.pad(fs[:, 1:2], ((0, 0), (0, seq - 1), (0, 0))).reshape(m, D_FF)
        p2 = jnp.pad(fs, ((0, 0), (0, seq - 2), (0, 0))).reshape(m, D_FF)
        act, g_up = _glu_short(h1_bf, p["w_gate"], p["w_up"], p1, p2, p["ffn_conv_w"], p["ffn_conv_b"], seq=seq)
        new_ffn = g_up.reshape(bsz, seq, D_FF)[:, seq - (FFN_CONV - 1):]
    y = _down_ln(act, p["w_down"], h1, p["ln2_g"], p["ln2_b"], alpha=alpha)
    return (y.reshape(bsz, seq, D_MODEL), k_new.reshape(bsz, seq, N_HEADS, HEAD_DIM),
            v_new.reshape(bsz, seq, N_HEADS, HEAD_DIM), h_new, new_conv, new_ffn)


def kernel(x_prompt, x_sample, cache_k, cache_v, state_ssm, state_conv, state_ffn_conv, page_table, w_in, ssm_conv_w, ssm_conv_b, dt_bias, a_log, d_skip, ssm_norm_g, w_proj_attn, w_proj_ssm, w_out, ln1_g, ln1_b, w_gate, w_up, ffn_conv_w, ffn_conv_b, w_down, ln2_g, ln2_b):
    depth = w_in.shape[0]
    alpha = (2.0 * depth) ** 0.25
    bp, lp = x_prompt.shape[0], x_prompt.shape[1]
    past_len = page_table.shape[1] * PAGE_SIZE
    pos_p = jnp.arange(lp, dtype=jnp.int32)
    pos_s = past_len + jnp.arange(x_sample.shape[1], dtype=jnp.int32)
    hp, hs = x_prompt, x_sample
    outs_p, outs_s = [], []
    for l in range(depth):
        p = _prep_layer(w_in[l], ssm_conv_w[l], ssm_conv_b[l], dt_bias[l], a_log[l], d_skip[l], ssm_norm_g[l],
                        w_proj_attn[l], w_proj_ssm[l], w_out[l], ln1_g[l], ln1_b[l], w_gate[l], w_up[l],
                        ffn_conv_w[l], ffn_conv_b[l], w_down[l], ln2_g[l], ln2_b[l])
        conv0 = jnp.zeros((bp, SSM_CONV - 1, CONV_DIM), F32)
        ssm0 = jnp.zeros((bp, SSM_HEADS, SSM_HEAD_DIM, SSM_STATE), F32)
        ffn0 = jnp.zeros((bp, FFN_CONV - 1, D_FF), F32)
        hp, *rest_p = _layer(hp, pos_p, p, conv_state=conv0, ssm_state=ssm0, ffn_state=ffn0, paged=None, alpha=alpha)
        hs, *rest_s = _layer(hs, pos_s, p, conv_state=state_conv[l], ssm_state=state_ssm[l],
                             ffn_state=state_ffn_conv[l], paged=(cache_k, cache_v, l, page_table), alpha=alpha)
        outs_p.append(rest_p)
        outs_s.append(rest_s)
    stack = lambda outs, i: jnp.stack([o[i] for o in outs])
    return (hp, hs, *(stack(outs_p, i) for i in range(5)), *(stack(outs_s, i) for i in range(5)))
```

```python
import functools

import jax
import jax.numpy as jnp
import numpy as np
from jax import lax
from jax.experimental import pallas as pl
from jax.experimental.pallas import tpu as pltpu

F32 = jnp.float32
BF16 = jnp.bfloat16
HIGHEST = lax.Precision.HIGHEST

D_MODEL = 2048
PAGE_SIZE = 128
N_HEADS = 8
HEAD_DIM = 128
ATTN_WIDTH = N_HEADS * HEAD_DIM
MOBA_BLOCK = 256
MOBA_BLOCK_SHIFT = 8
MOBA_TOPK = 3
ROPE_THETA = 10000.0
SSM_INNER = D_MODEL
SSM_HEAD_DIM = 64
SSM_HEADS = SSM_INNER // SSM_HEAD_DIM
SSM_GROUPS = 4
SSM_GROUP_HEADS = SSM_HEADS // SSM_GROUPS
SSM_GROUP_WIDTH = SSM_INNER // SSM_GROUPS
SSM_STATE = 128
SSM_CONV = 4
SSM_CHUNK = 128
CONV_DIM = SSM_INNER + 2 * SSM_GROUPS * SSM_STATE
D_FF = ((8 * D_MODEL // 3 + 255) // 256) * 256
FFN_CONV = 3
LN_EPS = 1e-5
RMS_EPS = 1e-5

LANES = 128
SUBLANES = 8
VMEM_LIMIT_BYTES = 56 * 1024 * 1024

NEG_BIG = -1e30
LOG2_E = 1.4426950408889634


def _cparams(*sem):
    return pltpu.CompilerParams(dimension_semantics=sem, vmem_limit_bytes=VMEM_LIMIT_BYTES)


def _tile(n, pref):
    t = min(pref, n)
    while n % t:
        t //= 2
    return t


def _nt_dot(a, b, precision=None):
    return lax.dot_general(a, b, (((1,), (1,)), ((), ())), precision=precision, preferred_element_type=F32)


def _proj_kernel(*refs, rope, with_bf16):
    a_ref, b_ref = refs[0:2]
    outs = refs[4:] if rope else refs[2:]
    acc = jnp.dot(a_ref[...], b_ref[...], preferred_element_type=F32)
    if rope:
        cos = refs[2][...]
        sin = refs[3][...]
    for h in range(acc.shape[1] // HEAD_DIM):
        cols = slice(h * HEAD_DIM, (h + 1) * HEAD_DIM)
        val = acc[:, cols]
        if rope:
            val = val * cos + pltpu.roll(val, HEAD_DIM // 2, 1) * sin
        outs[0][:, cols] = val
        if with_bf16:
            outs[1][:, cols] = val.astype(BF16)


def _project(a, b, rope=None, *, with_bf16=False, tm=1024, tn=512, name):
    m, k = a.shape
    n = b.shape[1]
    tm, tn = _tile(m, tm), _tile(n, tn)
    assert tn % HEAD_DIM == 0
    in_specs = [pl.BlockSpec((tm, k), lambda i, j: (i, 0)), pl.BlockSpec((k, tn), lambda i, j: (0, j))]
    if rope is not None:
        in_specs += [pl.BlockSpec((tm, HEAD_DIM), lambda i, j: (i, 0))] * 2
    out_spec = pl.BlockSpec((tm, tn), lambda i, j: (i, j))
    out = pl.pallas_call(
        functools.partial(_proj_kernel, rope=rope is not None, with_bf16=with_bf16),
        grid=(m // tm, n // tn),
        in_specs=in_specs,
        out_specs=[out_spec] * (2 if with_bf16 else 1),
        out_shape=[jax.ShapeDtypeStruct((m, n), F32)] + ([jax.ShapeDtypeStruct((m, n), BF16)] if with_bf16 else []),
        compiler_params=_cparams("parallel", "arbitrary"),
        name=name,
    )(a, b, *(rope or ()))
    return out if with_bf16 else out[0]


def _softplus(x):
    return jnp.maximum(x, 0.0) + jnp.log1p(jnp.exp(-jnp.abs(x)))


def _bf16_pieces(x):
    hi = x.astype(BF16)
    rest = x - hi.astype(F32)
    mid = rest.astype(BF16)
    return hi, mid, (rest - mid.astype(F32)).astype(BF16)


def _silu(x):
    return x * (0.5 * jnp.tanh(0.5 * x) + 0.5)


def _ssd_kernel(xbc_ref, z_ref, dt_ref, cst_ref, h0_ref, cw_ref, cb_ref, dtb_ref, alog_ref, dfull_ref, ng_ref, expand_ref,
                y_ref, hout_ref,
                ext_scr, xc_scr, dt_scr, ht_scr, yd_scr, yoff_scr, st_scr, *, lc, nchunks):
    q = SSM_CHUNK
    n = SSM_STATE
    c = pl.program_id(1)
    halo = SUBLANES

    @pl.when(c == 0)
    def _init():
        if lc < q:
            ext_scr[...] = jnp.zeros(ext_scr.shape, F32)
            dt_scr[...] = jnp.zeros(dt_scr.shape, F32)
        ext_scr[0:halo, :] = cst_ref[0]
        for g in range(SSM_GROUPS):
            ht_scr[g] = h0_ref[0, g].T

    ext_scr[halo:halo + lc, :] = xbc_ref[0]

    slab = 512
    row = lax.broadcasted_iota(jnp.int32, (q, slab), 0)
    for s in range(CONV_DIM // slab):
        cols = slice(s * slab, (s + 1) * slab)
        acc = ext_scr[halo - 3:halo - 3 + q, cols] * cw_ref[0:1, cols]
        for kk in range(1, SSM_CONV):
            acc = acc + ext_scr[halo - 3 + kk:halo - 3 + kk + q, cols] * cw_ref[kk:kk + 1, cols]
        act = _silu(acc + cb_ref[:, cols])
        if lc < q:
            act = jnp.where(row < lc, act, 0.0)
        xc_scr[:, cols] = act
    if nchunks > 1:
        ext_scr[0:halo, :] = ext_scr[lc:lc + halo, :]

    dt_new = _softplus(dt_ref[0] + dtb_ref[...])
    if lc < q:
        dt_scr[0:lc, :] = dt_new
        dt = dt_scr[...]
    else:
        dt = dt_new
    a = -jnp.exp(alog_ref[...])
    da = dt * a
    r_io = lax.broadcasted_iota(jnp.int32, (q, q), 0)
    c_io = lax.broadcasted_iota(jnp.int32, (q, q), 1)
    causal = r_io >= c_io
    tri = jnp.where(causal, 1.0, 0.0).astype(BF16)
    cs = sum(jnp.dot(tri, piece, preferred_element_type=F32) for piece in _bf16_pieces(da))
    cs_t = cs.T
    dt_t = dt.T
    e_exp = sum(jnp.dot(piece, expand_ref[...], preferred_element_type=F32) for piece in _bf16_pieces(jnp.exp(cs)))
    w_t = jnp.exp(cs_t[:, q - 1:q] - cs_t) * dt_t

    for g in range(SSM_GROUPS):
        b_g = xc_scr[:, SSM_INNER + g * n:SSM_INNER + (g + 1) * n]
        c_g = xc_scr[:, SSM_INNER + SSM_GROUPS * n + g * n:SSM_INNER + SSM_GROUPS * n + (g + 1) * n]
        c_bf = c_g.astype(BF16)
        cb = _nt_dot(c_bf, b_g.astype(BF16))
        b_t = b_g.T
        yoff_scr[:, g * SSM_GROUP_WIDTH:(g + 1) * SSM_GROUP_WIDTH] = jnp.dot(
            c_bf, ht_scr[g].astype(BF16), preferred_element_type=F32)
        for r in range(SSM_GROUP_HEADS):
            h = g * SSM_GROUP_HEADS + r
            cols = slice(h * SSM_HEAD_DIM, (h + 1) * SSM_HEAD_DIM)
            diff = cs[:, h:h + 1] - cs_t[h:h + 1, :]
            lmat = jnp.exp(jnp.where(causal, diff, -jnp.inf))
            m_h = (cb * lmat * dt_t[h:h + 1, :]).astype(BF16)
            x_h = xc_scr[:, cols].astype(BF16)
            yd_scr[:, cols] = jnp.dot(m_h, x_h, preferred_element_type=F32)
            bts = (b_t * w_t[h:h + 1, :]).astype(BF16)
            st_scr[:, r * SSM_HEAD_DIM:(r + 1) * SSM_HEAD_DIM] = jnp.dot(bts, x_h, preferred_element_type=F32)
        decay = e_exp[q - 1:q, g * SSM_GROUP_WIDTH:(g + 1) * SSM_GROUP_WIDTH]
        ht_scr[g] = ht_scr[g] * decay + st_scr[...]

    xs = xc_scr[0:lc, 0:SSM_INNER]
    y = yd_scr[0:lc, :] + yoff_scr[0:lc, :] * e_exp[0:lc, :] + dfull_ref[...] * xs
    y = y * _silu(z_ref[0])
    for g in range(SSM_GROUPS):
        cols = slice(g * SSM_GROUP_WIDTH, (g + 1) * SSM_GROUP_WIDTH)
        yg = y[:, cols]
        ms = jnp.mean(jnp.square(yg), axis=-1, keepdims=True)
        y_ref[0, :, cols] = (yg * lax.rsqrt(ms + RMS_EPS) * ng_ref[:, cols]).astype(y_ref.dtype)

    @pl.when(c == nchunks - 1)
    def _fin():
        for g in range(SSM_GROUPS):
            hout_ref[0, g] = ht_scr[g].T


def _ssd(src, dt_raw, conv_state, ssm_state, p, *, xbc_col, z_col, out_dtype):
    bsz, seq, _ = src.shape
    lc = min(seq, SSM_CHUNK)
    assert seq % lc == 0 and lc % SUBLANES == 0
    nchunks = seq // lc
    q = SSM_CHUNK
    const = lambda shape: pl.BlockSpec(shape, lambda b, c: (0,) * len(shape))
    y, h_last = pl.pallas_call(
        functools.partial(_ssd_kernel, lc=lc, nchunks=nchunks),
        grid=(bsz, nchunks),
        in_specs=[
            pl.BlockSpec((1, lc, CONV_DIM), lambda b, c: (b, c, xbc_col)),
            pl.BlockSpec((1, lc, SSM_INNER), lambda b, c: (b, c, z_col)),
            pl.BlockSpec((1, lc, LANES), lambda b, c: (b, c, 0)),
            pl.BlockSpec((1, SUBLANES, CONV_DIM), lambda b, c: (b, 0, 0)),
            pl.BlockSpec((1, SSM_GROUPS, SSM_GROUP_WIDTH, SSM_STATE), lambda b, c: (b, 0, 0, 0)),
            const((SSM_CONV, CONV_DIM)), const((1, CONV_DIM)), const((1, LANES)), const((1, LANES)),
            const((1, SSM_INNER)), const((1, SSM_INNER)), const((LANES, SSM_INNER)),
        ],
        out_specs=[
            pl.BlockSpec((1, lc, SSM_INNER), lambda b, c: (b, c, 0)),
            pl.BlockSpec((1, SSM_GROUPS, SSM_GROUP_WIDTH, SSM_STATE), lambda b, c: (b, 0, 0, 0)),
        ],
        out_shape=[jax.ShapeDtypeStruct((bsz, seq, SSM_INNER), out_dtype),
                   jax.ShapeDtypeStruct((bsz, SSM_GROUPS, SSM_GROUP_WIDTH, SSM_STATE), F32)],
        scratch_shapes=[
            pltpu.VMEM((SUBLANES + q, CONV_DIM), F32),
            pltpu.VMEM((q, CONV_DIM), F32),
            pltpu.VMEM((q, LANES), F32),
            pltpu.VMEM((SSM_GROUPS, SSM_STATE, SSM_GROUP_WIDTH), F32),
            pltpu.VMEM((q, SSM_INNER), F32),
            pltpu.VMEM((q, SSM_INNER), F32),
            pltpu.VMEM((SSM_STATE, SSM_GROUP_WIDTH), F32),
        ],
        compiler_params=_cparams("parallel", "arbitrary"),
        name="ssd",
    )(src, src, dt_raw, conv_state, ssm_state, p["conv_w"], p["conv_b"], p["dt_bias"], p["a_log"], p["d_full"],
      p["norm_g"], p["expand"])
    return y, h_last


def _select_blocks(gate, own, n_sel):
    blk = lax.broadcasted_iota(jnp.int32, gate.shape, 1)
    blk_f = blk.astype(F32)
    g = jnp.where(blk < own, gate, -jnp.inf)
    sel = jnp.zeros(gate.shape, F32)
    for t in range(n_sel):
        m = jnp.max(g, axis=1, keepdims=True)
        idx = jnp.min(jnp.where(g == m, blk_f, float(LANES)), axis=1, keepdims=True)
        hit = blk_f == idx
        sel = jnp.where(hit & (own > t), 1.0, sel)
        g = jnp.where(hit, -jnp.inf, g)
    return sel


def _kmean_kernel(k_ref, o_ref):
    rows = k_ref.shape[0]
    o_ref[...] = jnp.sum(k_ref[...].reshape(rows // MOBA_BLOCK, MOBA_BLOCK, k_ref.shape[1]), axis=1) / MOBA_BLOCK


def _block_means(k, *, nblk_pad):
    t = k.shape[0]
    nblk = t // MOBA_BLOCK
    per = _tile(nblk, SUBLANES)
    assert per == SUBLANES or per == nblk
    out = pl.pallas_call(
        _kmean_kernel,
        grid=(nblk // per,),
        in_specs=[pl.BlockSpec((per * MOBA_BLOCK, ATTN_WIDTH), lambda i: (i, 0))],
        out_specs=pl.BlockSpec((per, ATTN_WIDTH), lambda i: (i, 0)),
        out_shape=jax.ShapeDtypeStruct((nblk, ATTN_WIDTH), F32),
        compiler_params=_cparams("parallel"),
        name="moba_block_means",
    )(k)
    return jnp.pad(out, ((0, nblk_pad - nblk), (0, 0)))


def _attn_kernel(qi_ref, kj_ref, qf_ref, k_ref, koh_ref, v_ref, km_ref, o_ref,
                 qa_scr, m_scr, acc_scr, *, tile, sub, heads, n_sel, scale):
    p = pl.program_id(1)
    qi = qi_ref[p]
    kj = kj_ref[p]
    q_pos = qi * tile + lax.broadcasted_iota(jnp.int32, (tile, 1), 0)
    own = lax.shift_right_logical(q_pos, MOBA_BLOCK_SHIFT)
    head_cols = [slice(h * HEAD_DIM, (h + 1) * HEAD_DIM) for h in range(heads)]

    @pl.when(kj == 0)
    def _init():
        m_scr[...] = jnp.full(m_scr.shape, NEG_BIG, F32)
        acc_scr[...] = jnp.zeros(acc_scr.shape, F32)
        for h in range(heads):
            qf = qf_ref[:, head_cols[h]]
            gate = _nt_dot(qf, km_ref[:, head_cols[h]], precision=HIGHEST)
            sel = _select_blocks(gate, own, n_sel)
            blk = lax.broadcasted_iota(jnp.int32, sel.shape, 1)
            qa_scr[h, :, 0:HEAD_DIM] = (qf * (scale * LOG2_E)).astype(BF16)
            qa_scr[h, :, HEAD_DIM:] = jnp.where((sel > 0.5) | (blk == own), 0.0, NEG_BIG).astype(BF16)

    koh = koh_ref[...]
    ones = jnp.ones((tile, HEAD_DIM), BF16)
    k_aug = [jnp.concatenate([k_ref[:, head_cols[h]], koh], axis=1) for h in range(heads)]
    v_aug = [jnp.concatenate([v_ref[:, head_cols[h]], ones], axis=1) for h in range(heads)]

    def update(h, r0, n_keys, diag):
        rows = slice(r0, r0 + sub)
        s = _nt_dot(qa_scr[h, rows, :], k_aug[h][0:n_keys])
        if diag:
            k_pos = kj * tile + lax.broadcasted_iota(jnp.int32, (1, n_keys), 1)
            future = (lax.shift_right_logical(k_pos, MOBA_BLOCK_SHIFT) == own[rows]) & (k_pos > q_pos[rows])
            s = jnp.where(future, NEG_BIG, s)
        m_old = m_scr[h, rows, :]
        m_new = jnp.maximum(m_old, jnp.max(s, axis=1, keepdims=True))
        alpha = jnp.exp2(m_old - m_new)
        pr = jnp.exp2(s - m_new).astype(BF16)
        acc_scr[h, rows, :] = alpha * acc_scr[h, rows, :] + jnp.dot(pr, v_aug[h][0:n_keys],
                                                                    preferred_element_type=F32)
        m_scr[h, rows, :] = m_new

    @pl.when(kj < qi)
    def _past():
        for r0 in range(0, tile, sub):
            for h in range(heads):
                update(h, r0, tile, False)

    @pl.when(kj == qi)
    def _diag():
        for r0 in range(0, tile, sub):
            for h in range(heads):
                update(h, r0, r0 + sub, True)
        for h in range(heads):
            o_ref[:, head_cols[h]] = (acc_scr[h, :, 0:HEAD_DIM] / acc_scr[h, :, HEAD_DIM:]).astype(o_ref.dtype)


def _attend_prompt(q, k, k_bf, v_bf, *, out_dtype):
    t = q.shape[0]
    assert t % MOBA_BLOCK == 0
    nblk = t // MOBA_BLOCK
    assert nblk <= LANES
    n_sel = min(MOBA_TOPK, nblk)
    kmean = _block_means(k, nblk_pad=LANES)
    tile = _tile(t, 1024)
    assert tile % MOBA_BLOCK == 0
    pairs = [(i, j) for i in range(t // tile) for j in range(i + 1)]
    qi_arr = jnp.asarray([a for a, _ in pairs], jnp.int32)
    kj_arr = jnp.asarray([b for _, b in pairs], jnp.int32)
    k_onehot = (jnp.arange(t)[:, None] // MOBA_BLOCK == jnp.arange(LANES)[None, :]).astype(BF16)
    heads = 4
    width = heads * HEAD_DIM
    grid_spec = pltpu.PrefetchScalarGridSpec(
        num_scalar_prefetch=2,
        grid=(N_HEADS // heads, len(pairs)),
        in_specs=[
            pl.BlockSpec((tile, width), lambda h, p, qi, kj: (qi[p], h)),
            pl.BlockSpec((tile, width), lambda h, p, qi, kj: (kj[p], h)),
            pl.BlockSpec((tile, LANES), lambda h, p, qi, kj: (kj[p], 0)),
            pl.BlockSpec((tile, width), lambda h, p, qi, kj: (kj[p], h)),
            pl.BlockSpec((LANES, width), lambda h, p, qi, kj: (0, h)),
        ],
        out_specs=pl.BlockSpec((tile, width), lambda h, p, qi, kj: (qi[p], h)),
        scratch_shapes=[pltpu.VMEM((heads, tile, 2 * HEAD_DIM), BF16),
                        pltpu.VMEM((heads, tile, 1), F32),
                        pltpu.VMEM((heads, tile, 2 * HEAD_DIM), F32)],
    )
    return pl.pallas_call(
        functools.partial(_attn_kernel, tile=tile, sub=MOBA_BLOCK, heads=heads, n_sel=n_sel,
                          scale=HEAD_DIM ** -0.5),
        grid_spec=grid_spec,
        out_shape=jax.ShapeDtypeStruct((t, ATTN_WIDTH), out_dtype),
        compiler_params=_cparams("parallel", "arbitrary"),
        name="moba_prompt_attention",
    )(qi_arr, kj_arr, q, k_bf, k_onehot, v_bf, kmean)


def _page_head(page_ref, h):
    return page_ref[0, pl.ds(h, PAGE_SIZE, stride=N_HEADS), :]


def _paged_attn_kernel(pt_ref, *refs, gp, n_steps, dec, past, n_sel, scale):
    k_refs = refs[0:gp]
    v_refs = refs[gp:2 * gp]
    q_ref, kn_ref, vn_ref, o_ref = refs[2 * gp:2 * gp + 4]
    qbd_scr, kbf_scr, s_scr, ksum_scr, sel_scr, l_scr, oacc_scr, new_scr = refs[2 * gp + 4:]
    ph = pl.program_id(1)
    st = pl.program_id(2)
    page = PAGE_SIZE
    step_keys = gp * page
    t_past = past
    cols = N_HEADS * dec
    own = past // MOBA_BLOCK

    @pl.when((ph == 0) & (st == 0))
    def _init():
        new_scr[...] = jnp.zeros(new_scr.shape, F32)
        new_scr[0:dec, :] = q_ref[...]
        for h in range(N_HEADS):
            qt = new_scr[:, h * HEAD_DIM:(h + 1) * HEAD_DIM].T
            if h:
                qt = pltpu.roll(qt, h * dec, 1)
            qbd_scr[h * HEAD_DIM:(h + 1) * HEAD_DIM, :] = qt.astype(BF16)
        ksum_scr[...] = jnp.zeros(ksum_scr.shape, F32)

    @pl.when(ph == 0)
    def _scores():
        pages_per_blk = MOBA_BLOCK // page
        blks_per_step = gp // pages_per_blk
        sums = []
        for g in range(0, gp, pages_per_blk):
            for h in range(N_HEADS):
                tot = None
                for gg in range(g, g + pages_per_blk):
                    kh = _page_head(k_refs[gg], h)
                    kbf_scr[gg * page:(gg + 1) * page, h * HEAD_DIM:(h + 1) * HEAD_DIM] = kh.astype(BF16)
                    part = jnp.sum(kh, axis=0, keepdims=True)
                    tot = part if tot is None else tot + part
                sums.append(tot)
            s_blk = jnp.dot(kbf_scr[g * page:(g + pages_per_blk) * page, :], qbd_scr[...],
                            preferred_element_type=F32) * scale
            s_scr[pl.ds(pl.multiple_of(st * step_keys + g * page, MOBA_BLOCK), MOBA_BLOCK), :] = s_blk
        for s_static in range(n_steps):
            @pl.when(st == s_static)
            def _store_sums(s_static=s_static):
                for j, tot in enumerate(sums):
                    blk = s_static * blks_per_step + j // N_HEADS
                    h = j % N_HEADS
                    ksum_scr[blk:blk + 1, h * HEAD_DIM:(h + 1) * HEAD_DIM] = tot

    @pl.when((ph == 0) & (st == n_steps - 1))
    def _softmax():
        new_scr[...] = jnp.zeros(new_scr.shape, F32)
        new_scr[0:dec, :] = kn_ref[...]
        s_scr[t_past:t_past + page, :] = jnp.dot(new_scr[...].astype(BF16), qbd_scr[...],
                                                 preferred_element_type=F32) * scale
        own_col = jnp.full((dec, 1), own, jnp.int32)
        sel_scr[...] = jnp.zeros(sel_scr.shape, F32)
        for h in range(N_HEADS):
            kmean = ksum_scr[:, h * HEAD_DIM:(h + 1) * HEAD_DIM] / MOBA_BLOCK
            gate = _nt_dot(q_ref[:, h * HEAD_DIM:(h + 1) * HEAD_DIM], kmean, precision=HIGHEST)
            sel_scr[h * dec:(h + 1) * dec, :] = _select_blocks(gate, own_col, n_sel)
        sel_bf = sel_scr[...].astype(BF16)
        col = lax.broadcasted_iota(jnp.int32, (page, LANES), 1)
        krow = lax.broadcasted_iota(jnp.int32, (page, LANES), 0)
        new_ok = (krow <= col % dec) & (col < cols)
        s_new = jnp.where(new_ok, s_scr[t_past:t_past + page, :], NEG_BIG)
        s_scr[t_past:t_past + page, :] = s_new
        blk_col = lax.broadcasted_iota(jnp.int32, (step_keys, LANES), 1)
        key_row = lax.broadcasted_iota(jnp.int32, (step_keys, LANES), 0)

        def mask_body(ci, m):
            rows = pl.ds(pl.multiple_of(ci * step_keys, step_keys), step_keys)
            onehot = (lax.shift_right_logical(ci * step_keys + key_row, MOBA_BLOCK_SHIFT) == blk_col).astype(BF16)
            picked = _nt_dot(onehot, sel_bf)
            sc = jnp.where(picked > 0.5, s_scr[rows, :], NEG_BIG)
            s_scr[rows, :] = sc
            return jnp.maximum(m, jnp.max(sc, axis=0, keepdims=True))

        m = lax.fori_loop(0, n_steps, mask_body, jnp.max(s_new, axis=0, keepdims=True))

        def exp_body(ci, carry):
            rows = pl.ds(pl.multiple_of(ci * step_keys, step_keys), step_keys)
            s_scr[rows, :] = jnp.exp(s_scr[rows, :] - m)
            return carry

        lax.fori_loop(0, n_steps, exp_body, 0)
        s_scr[t_past:t_past + page, :] = jnp.exp(s_new - m)
        l_scr[...] = jnp.zeros(l_scr.shape, F32)
        oacc_scr[...] = jnp.zeros(oacc_scr.shape, F32)

    @pl.when(ph == 1)
    def _values():
        pages_per_blk = MOBA_BLOCK // page
        l_tot = None
        o_tot = None
        for g in range(0, gp, pages_per_blk):
            for gg in range(g, g + pages_per_blk):
                for h in range(N_HEADS):
                    kbf_scr[gg * page:(gg + 1) * page, h * HEAD_DIM:(h + 1) * HEAD_DIM] = _page_head(
                        v_refs[gg], h).astype(BF16)
            rows = pl.ds(pl.multiple_of(st * step_keys + g * page, MOBA_BLOCK), MOBA_BLOCK)
            pr = s_scr[rows, :].T
            l_blk = jnp.sum(pr, axis=1, keepdims=True)
            o_blk = jnp.dot(pr.astype(BF16), kbf_scr[g * page:(g + pages_per_blk) * page, :],
                            preferred_element_type=F32)
            l_tot = l_blk if l_tot is None else l_tot + l_blk
            o_tot = o_blk if o_tot is None else o_tot + o_blk
        l_scr[...] += l_tot
        oacc_scr[...] += o_tot

    @pl.when((ph == 1) & (st == n_steps - 1))
    def _fin():
        new_scr[...] = jnp.zeros(new_scr.shape, F32)
        new_scr[0:dec, :] = vn_ref[...]
        pr = s_scr[t_past:t_past + page, :].T
        l_tot = l_scr[...] + jnp.sum(pr, axis=1, keepdims=True)
        o_full = oacc_scr[...] + jnp.dot(pr.astype(BF16), new_scr[...].astype(BF16), preferred_element_type=F32)
        for h in range(N_HEADS):
            o_ref[:, h * HEAD_DIM:(h + 1) * HEAD_DIM] = (
                o_full[h * dec:(h + 1) * dec, h * HEAD_DIM:(h + 1) * HEAD_DIM] / l_tot[h * dec:(h + 1) * dec, :]
            ).astype(o_ref.dtype)


def _attend_paged(cache_k, cache_v, layer, page_table, q, k_new, v_new, *, dec):
    bsz, n_pages = page_table.shape
    past = n_pages * PAGE_SIZE
    assert past % MOBA_BLOCK == 0 and dec == SUBLANES and N_HEADS * dec <= LANES
    own = past // MOBA_BLOCK
    assert own + 1 <= LANES
    n_sel = min(MOBA_TOPK, own + 1)
    depth, n_pool = cache_k.shape[0], cache_k.shape[1]
    ck = cache_k.reshape(depth * n_pool, PAGE_SIZE * N_HEADS, HEAD_DIM)
    cv = cache_v.reshape(depth * n_pool, PAGE_SIZE * N_HEADS, HEAD_DIM)
    first = layer * n_pool
    gp = _tile(n_pages, 16)
    assert gp % (MOBA_BLOCK // PAGE_SIZE) == 0
    n_steps = n_pages // gp
    t_all = past + PAGE_SIZE

    def k_map(g):
        return lambda b, ph, st, pt: (first + pt[b, jnp.where(ph == 0, st, n_steps - 1) * gp + g], 0, 0)

    def v_map(g):
        return lambda b, ph, st, pt: (first + pt[b, jnp.where(ph == 1, st, 0) * gp + g], 0, 0)

    page_spec = lambda fn: pl.BlockSpec((1, PAGE_SIZE * N_HEADS, HEAD_DIM), fn)
    row_spec = pl.BlockSpec((dec, ATTN_WIDTH), lambda b, ph, st, pt: (b, 0))
    grid_spec = pltpu.PrefetchScalarGridSpec(
        num_scalar_prefetch=1,
        grid=(bsz, 2, n_steps),
        in_specs=[page_spec(k_map(g)) for g in range(gp)] + [page_spec(v_map(g)) for g in range(gp)]
        + [row_spec] * 3,
        out_specs=row_spec,
        scratch_shapes=[
            pltpu.VMEM((ATTN_WIDTH, LANES), BF16),
            pltpu.VMEM((gp * PAGE_SIZE, ATTN_WIDTH), BF16),
            pltpu.VMEM((t_all, LANES), F32),
            pltpu.VMEM((LANES, ATTN_WIDTH), F32),
            pltpu.VMEM((LANES, LANES), F32),
            pltpu.VMEM((LANES, 1), F32),
            pltpu.VMEM((LANES, ATTN_WIDTH), F32),
            pltpu.VMEM((PAGE_SIZE, ATTN_WIDTH), F32),
        ],
    )
    return pl.pallas_call(
        functools.partial(_paged_attn_kernel, gp=gp, n_steps=n_steps, dec=dec, past=past, n_sel=n_sel,
                          scale=HEAD_DIM ** -0.5),
        grid_spec=grid_spec,
        out_shape=jax.ShapeDtypeStruct((bsz * dec, ATTN_WIDTH), F32),
        compiler_params=_cparams("parallel", "arbitrary", "arbitrary"),
        name="moba_paged_attention",
    )(page_table, *([ck] * gp), *([cv] * gp), q, k_new, v_new)


def _layer_norm(x, g, b):
    mu = jnp.mean(x, axis=-1, keepdims=True)
    xc = x - mu
    var = jnp.mean(jnp.square(xc), axis=-1, keepdims=True)
    return xc * lax.rsqrt(var + LN_EPS) * g + b


def _merge_ln_kernel(oa_ref, ys_ref, ga_ref, gs_ref, res_ref, wa_ref, ws_ref, wo_ref, g_ref, b_ref, o_ref, obf_ref,
                     *, alpha, sub):
    for r0 in range(0, oa_ref.shape[0], sub):
        rows = slice(r0, r0 + sub)
        pa = jnp.dot(oa_ref[rows, :].astype(BF16), wa_ref[...], preferred_element_type=F32)
        ps = jnp.dot(ys_ref[rows, :].astype(BF16), ws_ref[...], preferred_element_type=F32)
        merged = jax.nn.sigmoid(ga_ref[rows, :]) * pa + jax.nn.sigmoid(gs_ref[rows, :]) * ps
        acc = jnp.dot(merged.astype(BF16), wo_ref[...], preferred_element_type=F32)
        y = _layer_norm(alpha * res_ref[rows, :] + acc, g_ref[...], b_ref[...])
        o_ref[rows, :] = y
        obf_ref[rows, :] = y.astype(BF16)


def _merge_ln(o_attn, y_s, rest, res, w_pa, w_ps, w_out, g, b, *, ga_col, gs_col, alpha, tm=256):
    m = o_attn.shape[0]
    tm = _tile(m, tm)
    row = lambda width, col=0: pl.BlockSpec((tm, width), lambda i: (i, col))
    resident = lambda shape: pl.BlockSpec(shape, lambda i: (0, 0), pipeline_mode=pl.Buffered(1))
    return pl.pallas_call(
        functools.partial(_merge_ln_kernel, alpha=alpha, sub=_tile(tm, 128)),
        grid=(m // tm,),
        in_specs=[row(ATTN_WIDTH), row(SSM_INNER), row(D_MODEL, ga_col), row(D_MODEL, gs_col), row(D_MODEL),
                  resident((ATTN_WIDTH, D_MODEL)), resident((SSM_INNER, D_MODEL)), resident((D_MODEL, D_MODEL)),
                  resident((1, D_MODEL)), resident((1, D_MODEL))],
        out_specs=[row(D_MODEL), row(D_MODEL)],
        out_shape=[jax.ShapeDtypeStruct((m, D_MODEL), F32), jax.ShapeDtypeStruct((m, D_MODEL), BF16)],
        compiler_params=_cparams("parallel"),
        name="merge_out_proj_layernorm",
    )(o_attn, y_s, rest, rest, res, w_pa, w_ps, w_out, g, b)


def _gelu(x):
    return 0.5 * x * (1.0 + lax.erf(x * np.float32(np.sqrt(0.5))))


def _glu_long_kernel(h_ref, wg_ref, wu_ref, st_ref, cw_ref, cb_ref, act_ref, tail_ref, ext_scr, *, tm, n_row_tiles):
    i = pl.program_id(1)
    halo = SUBLANES

    @pl.when(i == 0)
    def _init():
        ext_scr[0:halo, :] = st_ref[...]

    sub = _tile(tm, 256)
    for r0 in range(0, tm, sub):
        hh = h_ref[r0:r0 + sub, :]
        g_up = jnp.dot(hh, wg_ref[...], preferred_element_type=F32)
        u = jnp.dot(hh, wu_ref[...], preferred_element_type=F32)
        ext_scr[halo + r0:halo + r0 + sub, :] = g_up
        first = halo - (FFN_CONV - 1) + r0
        acc = ext_scr[first:first + sub, :] * cw_ref[0:1, :]
        for kk in range(1, FFN_CONV):
            acc = acc + ext_scr[first + kk:first + kk + sub, :] * cw_ref[kk:kk + 1, :]
        act_ref[r0:r0 + sub, :] = (_gelu(acc + cb_ref[...]) * u).astype(act_ref.dtype)
    ext_scr[0:halo, :] = ext_scr[tm:tm + halo, :]

    @pl.when(i == n_row_tiles - 1)
    def _tail():
        tail_ref[...] = ext_scr[tm:tm + halo, :]


def _glu_long(h_bf, w_gate, w_up, state, conv_w, conv_b, *, tm=1024, tn=512):
    m, k = h_bf.shape
    tm = _tile(m, tm)
    n_row_tiles = m // tm
    return pl.pallas_call(
        functools.partial(_glu_long_kernel, tm=tm, n_row_tiles=n_row_tiles),
        grid=(D_FF // tn, n_row_tiles),
        in_specs=[pl.BlockSpec((tm, k), lambda j, i: (i, 0)), pl.BlockSpec((k, tn), lambda j, i: (0, j)),
                  pl.BlockSpec((k, tn), lambda j, i: (0, j)), pl.BlockSpec((SUBLANES, tn), lambda j, i: (0, j)),
                  pl.BlockSpec((FFN_CONV, tn), lambda j, i: (0, j)), pl.BlockSpec((1, tn), lambda j, i: (0, j))],
        out_specs=[pl.BlockSpec((tm, tn), lambda j, i: (i, j)), pl.BlockSpec((SUBLANES, tn), lambda j, i: (0, j))],
        out_shape=[jax.ShapeDtypeStruct((m, D_FF), BF16), jax.ShapeDtypeStruct((SUBLANES, D_FF), F32)],
        scratch_shapes=[pltpu.VMEM((SUBLANES + tm, tn), F32)],
        compiler_params=_cparams("parallel", "arbitrary"),
        name="convglu_up_long",
    )(h_bf, w_gate, w_up, state, conv_w, conv_b)


def _glu_short_kernel(h_ref, wg_ref, wu_ref, p1_ref, p2_ref, cw_ref, cb_ref, act_ref, gup_ref, *, seq):
    hh = h_ref[...]
    g_up = jnp.dot(hh, wg_ref[...], preferred_element_type=F32)
    u = jnp.dot(hh, wu_ref[...], preferred_element_type=F32)
    gup_ref[...] = g_up
    r = lax.broadcasted_iota(jnp.int32, g_up.shape, 0) % seq
    back1 = jnp.where(r >= 1, pltpu.roll(g_up, 1, 0), p1_ref[...])
    back2 = jnp.where(r >= 2, pltpu.roll(g_up, 2, 0), p2_ref[...])
    acc = back2 * cw_ref[0:1, :] + back1 * cw_ref[1:2, :] + g_up * cw_ref[2:3, :]
    act_ref[...] = (_gelu(acc + cb_ref[...]) * u).astype(act_ref.dtype)


def _glu_short(h_bf, w_gate, w_up, p1, p2, conv_w, conv_b, *, seq, tn=512):
    m, k = h_bf.shape
    full = lambda j: (0, j)
    return pl.pallas_call(
        functools.partial(_glu_short_kernel, seq=seq),
        grid=(D_FF // tn,),
        in_specs=[pl.BlockSpec((m, k), lambda j: (0, 0)), pl.BlockSpec((k, tn), full), pl.BlockSpec((k, tn), full),
                  pl.BlockSpec((m, tn), full), pl.BlockSpec((m, tn), full),
                  pl.BlockSpec((FFN_CONV, tn), full), pl.BlockSpec((1, tn), full)],
        out_specs=[pl.BlockSpec((m, tn), full), pl.BlockSpec((m, tn), full)],
        out_shape=[jax.ShapeDtypeStruct((m, D_FF), BF16), jax.ShapeDtypeStruct((m, D_FF), F32)],
        compiler_params=_cparams("parallel"),
        name="convglu_up_short",
    )(h_bf, w_gate, w_up, p1, p2, conv_w, conv_b)


def _down_ln_kernel(a_ref, w_ref, res_ref, g_ref, b_ref, o_ref, acc_scr, *, alpha, n_k):
    kk = pl.program_id(1)

    @pl.when(kk == 0)
    def _init():
        acc_scr[...] = jnp.zeros(acc_scr.shape, F32)

    acc_scr[...] += jnp.dot(a_ref[...], w_ref[...], preferred_element_type=F32)

    @pl.when(kk == n_k - 1)
    def _fin():
        o_ref[...] = _layer_norm(alpha * res_ref[...] + acc_scr[...], g_ref[...], b_ref[...])


def _down_ln(a, w, res, g, b, *, alpha, tm=512, tk=D_FF // 4):
    m, k = a.shape
    n = w.shape[1]
    tm = _tile(m, tm)
    n_k = k // tk
    return pl.pallas_call(
        functools.partial(_down_ln_kernel, alpha=alpha, n_k=n_k),
        grid=(m // tm, n_k),
        in_specs=[pl.BlockSpec((tm, tk), lambda i, kk: (i, kk)), pl.BlockSpec((tk, n), lambda i, kk: (kk, 0)),
                  pl.BlockSpec((tm, n), lambda i, kk: (i, 0)), pl.BlockSpec((1, n), lambda i, kk: (0, 0)),
                  pl.BlockSpec((1, n), lambda i, kk: (0, 0))],
        out_specs=pl.BlockSpec((tm, n), lambda i, kk: (i, 0)),
        out_shape=jax.ShapeDtypeStruct((m, n), F32),
        scratch_shapes=[pltpu.VMEM((tm, n), F32)],
        compiler_params=_cparams("parallel", "arbitrary"),
        name="down_proj_layernorm",
    )(a, w, res, g, b)


REST_Z, REST_GA, REST_GS = 0, 1, 2
REST_XBC = 3 * D_MODEL // CONV_DIM
assert REST_XBC * CONV_DIM == 3 * D_MODEL


def _prep_layer(w_in, ssm_conv_w, ssm_conv_b, dt_bias, a_log, d_skip, ssm_norm_g, w_proj_attn, w_proj_ssm, w_out,
                ln1_g, ln1_b, w_gate, w_up, ffn_conv_w, ffn_conv_b, w_down, ln2_g, ln2_b):
    widths = (ATTN_WIDTH, ATTN_WIDTH, ATTN_WIDTH, SSM_INNER, CONV_DIM, SSM_HEADS, D_MODEL, D_MODEL)
    offs = np.concatenate([[0], np.cumsum(widths)])
    wq, wk, wv, wz, wxbc, wdt, wga, wgs = (w_in[:, offs[i]:offs[i + 1]] for i in range(8))
    row = lambda v: v.reshape(1, -1).astype(F32)
    pad_lanes = lambda v: jnp.pad(row(v), ((0, 0), (0, LANES - v.shape[-1])))
    return {
        "w_q": wq.astype(BF16), "w_k": wk.astype(BF16), "w_v": wv.astype(BF16),
        "w_rest": jnp.concatenate([wz, wga, wgs, wxbc], axis=1).astype(BF16),
        "w_dt": jnp.pad(wdt, ((0, 0), (0, LANES - SSM_HEADS))).astype(BF16),
        "conv_w": ssm_conv_w.astype(F32), "conv_b": row(ssm_conv_b),
        "dt_bias": pad_lanes(dt_bias), "a_log": pad_lanes(a_log),
        "d_full": row(jnp.repeat(d_skip, SSM_HEAD_DIM)), "norm_g": row(ssm_norm_g),
        "expand": (jnp.arange(LANES)[:, None] == jnp.arange(SSM_INNER)[None, :] // SSM_HEAD_DIM).astype(BF16),
        "w_pa": w_proj_attn.astype(BF16), "w_ps": w_proj_ssm.astype(BF16), "w_out": w_out.astype(BF16),
        "ln1_g": row(ln1_g), "ln1_b": row(ln1_b),
        "w_gate": w_gate.astype(BF16), "w_up": w_up.astype(BF16),
        "ffn_conv_w": ffn_conv_w.astype(F32), "ffn_conv_b": row(ffn_conv_b),
        "w_down": w_down.astype(BF16), "ln2_g": row(ln2_g), "ln2_b": row(ln2_b),
    }


def _rope_tables(pos):
    half = HEAD_DIM // 2
    inv = ROPE_THETA ** (-jnp.arange(half, dtype=F32) * 2.0 / HEAD_DIM)
    ang = pos.astype(F32)[:, None] * inv[None, :]
    cos, sin = jnp.cos(ang), jnp.sin(ang)
    return jnp.concatenate([cos, cos], axis=1), jnp.concatenate([-sin, sin], axis=1)


def _layer(x, pos, p, *, conv_state, ssm_state, ffn_state, paged, alpha):
    bsz, seq, _ = x.shape
    m = bsz * seq
    x2 = x.reshape(m, D_MODEL)
    x_bf = x2.astype(BF16)
    cos, sin = _rope_tables(pos)
    if bsz > 1:
        cos, sin = jnp.tile(cos, (bsz, 1)), jnp.tile(sin, (bsz, 1))
    prompt = paged is None
    q = _project(x_bf, p["w_q"], (cos, sin), name="in_proj_q_rope")
    k_new = _project(x_bf, p["w_k"], (cos, sin), with_bf16=prompt, name="in_proj_k_rope")
    v_new = _project(x_bf, p["w_v"], with_bf16=prompt, name="in_proj_v")
    if prompt:
        (k_new, k_bf), (v_new, v_bf) = k_new, v_new
    rest = _project(x_bf, p["w_rest"], tn=1024, name="in_proj_rest")
    dt_raw = _project(x_bf, p["w_dt"], tn=LANES, name="in_proj_dt")

    assert seq >= SSM_CONV - 1
    new_conv = rest.reshape(bsz, seq, -1)[:, seq - (SSM_CONV - 1):, REST_XBC * CONV_DIM:(REST_XBC + 1) * CONV_DIM]

    if prompt:
        assert bsz == 1
        o_attn = _attend_prompt(q, k_new, k_bf, v_bf, out_dtype=BF16)
    else:
        cache_k, cache_v, layer, page_table = paged
        o_attn = _attend_paged(cache_k, cache_v, layer, page_table, q, k_new, v_new, dec=seq)

    cst = jnp.pad(conv_state.astype(F32), ((0, 0), (SUBLANES - (SSM_CONV - 1), 0), (0, 0)))
    h0 = ssm_state.astype(F32).reshape(bsz, SSM_GROUPS, SSM_GROUP_WIDTH, SSM_STATE)
    y_s, h_last = _ssd(rest.reshape(bsz, seq, -1), dt_raw.reshape(bsz, seq, LANES), cst, h0, p,
                       xbc_col=REST_XBC, z_col=REST_Z, out_dtype=BF16 if seq % 16 == 0 else F32)
    h_new = h_last.reshape(bsz, SSM_HEADS, SSM_HEAD_DIM, SSM_STATE)

    h1, h1_bf = _merge_ln(o_attn, y_s.reshape(m, SSM_INNER), rest, x2, p["w_pa"], p["w_ps"], p["w_out"],
                          p["ln1_g"], p["ln1_b"], ga_col=REST_GA, gs_col=REST_GS, alpha=alpha)

    if bsz == 1:
        st = jnp.pad(ffn_state[0].astype(F32), ((SUBLANES - (FFN_CONV - 1), 0), (0, 0)))
        act, tail = _glu_long(h1_bf, p["w_gate"], p["w_up"], st, p["ffn_conv_w"], p["ffn_conv_b"])
        assert seq >= FFN_CONV - 1
        new_ffn = tail[None, SUBLANES - (FFN_CONV - 1):]
    else:
        assert seq == SUBLANES
        fs = ffn_state.astype(F32)
        p1 = jnp.pad(fs[:, 1:2], ((0, 0), (0, seq - 1), (0, 0))).reshape(m, D_FF)
        p2 = jnp.pad(fs, ((0, 0), (0, seq - 2), (0, 0))).reshape(m, D_FF)
        act, g_up = _glu_short(h1_bf, p["w_gate"], p["w_up"], p1, p2, p["ffn_conv_w"], p["ffn_conv_b"], seq=seq)
        new_ffn = g_up.reshape(bsz, seq, D_FF)[:, seq - (FFN_CONV - 1):]
    y = _down_ln(act, p["w_down"], h1, p["ln2_g"], p["ln2_b"], alpha=alpha)
    return (y.reshape(bsz, seq, D_MODEL), k_new.reshape(bsz, seq, N_HEADS, HEAD_DIM),
            v_new.reshape(bsz, seq, N_HEADS, HEAD_DIM), h_new, new_conv, new_ffn)


def kernel(x_prompt, x_sample, cache_k, cache_v, state_ssm, state_conv, state_ffn_conv, page_table, w_in, ssm_conv_w, ssm_conv_b, dt_bias, a_log, d_skip, ssm_norm_g, w_proj_attn, w_proj_ssm, w_out, ln1_g, ln1_b, w_gate, w_up, ffn_conv_w, ffn_conv_b, w_down, ln2_g, ln2_b):
    depth = w_in.shape[0]
    alpha = (2.0 * depth) ** 0.25
    bp, lp = x_prompt.shape[0], x_prompt.shape[1]
    past_len = page_table.shape[1] * PAGE_SIZE
    pos_p = jnp.arange(lp, dtype=jnp.int32)
    pos_s = past_len + jnp.arange(x_sample.shape[1], dtype=jnp.int32)
    hp, hs = x_prompt, x_sample
    outs_p, outs_s = [], []
    for l in range(depth):
        p = _prep_layer(w_in[l], ssm_conv_w[l], ssm_conv_b[l], dt_bias[l], a_log[l], d_skip[l], ssm_norm_g[l],
                        w_proj_attn[l], w_proj_ssm[l], w_out[l], ln1_g[l], ln1_b[l], w_gate[l], w_up[l],
                        ffn_conv_w[l], ffn_conv_b[l], w_down[l], ln2_g[l], ln2_b[l])
        conv0 = jnp.zeros((bp, SSM_CONV - 1, CONV_DIM), F32)
        ssm0 = jnp.zeros((bp, SSM_HEADS, SSM_HEAD_DIM, SSM_STATE), F32)
        ffn0 = jnp.zeros((bp, FFN_CONV - 1, D_FF), F32)
        hp, *rest_p = _layer(hp, pos_p, p, conv_state=conv0, ssm_state=ssm0, ffn_state=ffn0, paged=None, alpha=alpha)
        hs, *rest_s = _layer(hs, pos_s, p, conv_state=state_conv[l], ssm_state=state_ssm[l],
                             ffn_state=state_ffn_conv[l], paged=(cache_k, cache_v, l, page_table), alpha=alpha)
        outs_p.append(rest_p)
        outs_s.append(rest_s)
    stack = lambda outs, i: jnp.stack([o[i] for o in outs])
    return (hp, hs, *(stack(outs_p, i) for i in range(5)), *(stack(outs_s, i) for i in range(5)))
```

```python
import functools

import jax
import jax.numpy as jnp
import numpy as np
from jax import lax
from jax.experimental import pallas as pl
from jax.experimental.pallas import tpu as pltpu

F32 = jnp.float32
BF16 = jnp.bfloat16
HIGHEST = lax.Precision.HIGHEST

D_MODEL = 2048
PAGE_SIZE = 128
N_HEADS = 8
HEAD_DIM = 128
ATTN_WIDTH = N_HEADS * HEAD_DIM
MOBA_BLOCK = 256
MOBA_BLOCK_SHIFT = 8
MOBA_TOPK = 3
ROPE_THETA = 10000.0
SSM_INNER = D_MODEL
SSM_HEAD_DIM = 64
SSM_HEADS = SSM_INNER // SSM_HEAD_DIM
SSM_GROUPS = 4
SSM_GROUP_HEADS = SSM_HEADS // SSM_GROUPS
SSM_GROUP_WIDTH = SSM_INNER // SSM_GROUPS
SSM_STATE = 128
SSM_CONV = 4
SSM_CHUNK = 128
CONV_DIM = SSM_INNER + 2 * SSM_GROUPS * SSM_STATE
D_FF = ((8 * D_MODEL // 3 + 255) // 256) * 256
FFN_CONV = 3
LN_EPS = 1e-5
RMS_EPS = 1e-5

LANES = 128
SUBLANES = 8
VMEM_LIMIT_BYTES = 56 * 1024 * 1024

NEG_BIG = -1e30
LOG2_E = 1.4426950408889634


def _cparams(*sem):
    return pltpu.CompilerParams(dimension_semantics=sem, vmem_limit_bytes=VMEM_LIMIT_BYTES)


def _tile(n, pref):
    t = min(pref, n)
    while n % t:
        t //= 2
    return t


def _nt_dot(a, b, precision=None):
    return lax.dot_general(a, b, (((1,), (1,)), ((), ())), precision=precision, preferred_element_type=F32)


def _proj_kernel(*refs, rope, with_bf16):
    a_ref, b_ref = refs[0:2]
    outs = refs[4:] if rope else refs[2:]
    acc = jnp.dot(a_ref[...], b_ref[...], preferred_element_type=F32)
    if rope:
        cos = refs[2][...]
        sin = refs[3][...]
    for h in range(acc.shape[1] // HEAD_DIM):
        cols = slice(h * HEAD_DIM, (h + 1) * HEAD_DIM)
        val = acc[:, cols]
        if rope:
            val = val * cos + pltpu.roll(val, HEAD_DIM // 2, 1) * sin
        outs[0][:, cols] = val
        if with_bf16:
            outs[1][:, cols] = val.astype(BF16)


def _project(a, b, rope=None, *, with_bf16=False, tm=1024, tn=512, name):
    m, k = a.shape
    n = b.shape[1]
    tm, tn = _tile(m, tm), _tile(n, tn)
    assert tn % HEAD_DIM == 0
    in_specs = [pl.BlockSpec((tm, k), lambda i, j: (i, 0)), pl.BlockSpec((k, tn), lambda i, j: (0, j))]
    if rope is not None:
        in_specs += [pl.BlockSpec((tm, HEAD_DIM), lambda i, j: (i, 0))] * 2
    out_spec = pl.BlockSpec((tm, tn), lambda i, j: (i, j))
    out = pl.pallas_call(
        functools.partial(_proj_kernel, rope=rope is not None, with_bf16=with_bf16),
        grid=(m // tm, n // tn),
        in_specs=in_specs,
        out_specs=[out_spec] * (2 if with_bf16 else 1),
        out_shape=[jax.ShapeDtypeStruct((m, n), F32)] + ([jax.ShapeDtypeStruct((m, n), BF16)] if with_bf16 else []),
        compiler_params=_cparams("parallel", "arbitrary"),
        name=name,
    )(a, b, *(rope or ()))
    return out if with_bf16 else out[0]


def _softplus(x):
    return jnp.maximum(x, 0.0) + jnp.log1p(jnp.exp(-jnp.abs(x)))


def _bf16_pieces(x):
    hi = x.astype(BF16)
    rest = x - hi.astype(F32)
    mid = rest.astype(BF16)
    return hi, mid, (rest - mid.astype(F32)).astype(BF16)


def _silu(x):
    return x * (0.5 * jnp.tanh(0.5 * x) + 0.5)


def _ssd_kernel(xbc_ref, z_ref, dt_ref, cst_ref, h0_ref, cw_ref, cb_ref, dtb_ref, alog_ref, dfull_ref, ng_ref, expand_ref,
                y_ref, hout_ref,
                ext_scr, xc_scr, dt_scr, ht_scr, yd_scr, yoff_scr, st_scr, *, lc, nchunks):
    q = SSM_CHUNK
    n = SSM_STATE
    c = pl.program_id(1)
    halo = SUBLANES

    @pl.when(c == 0)
    def _init():
        if lc < q:
            ext_scr[...] = jnp.zeros(ext_scr.shape, F32)
            dt_scr[...] = jnp.zeros(dt_scr.shape, F32)
        ext_scr[0:halo, :] = cst_ref[0]
        for g in range(SSM_GROUPS):
            ht_scr[g] = h0_ref[0, g].T

    ext_scr[halo:halo + lc, :] = xbc_ref[0]

    slab = 512
    row = lax.broadcasted_iota(jnp.int32, (q, slab), 0)
    for s in range(CONV_DIM // slab):
        cols = slice(s * slab, (s + 1) * slab)
        acc = ext_scr[halo - 3:halo - 3 + q, cols] * cw_ref[0:1, cols]
        for kk in range(1, SSM_CONV):
            acc = acc + ext_scr[halo - 3 + kk:halo - 3 + kk + q, cols] * cw_ref[kk:kk + 1, cols]
        act = _silu(acc + cb_ref[:, cols])
        if lc < q:
            act = jnp.where(row < lc, act, 0.0)
        xc_scr[:, cols] = act
    if nchunks > 1:
        ext_scr[0:halo, :] = ext_scr[lc:lc + halo, :]

    dt_new = _softplus(dt_ref[0] + dtb_ref[...])
    if lc < q:
        dt_scr[0:lc, :] = dt_new
        dt = dt_scr[...]
    else:
        dt = dt_new
    a = -jnp.exp(alog_ref[...])
    da = dt * a
    r_io = lax.broadcasted_iota(jnp.int32, (q, q), 0)
    c_io = lax.broadcasted_iota(jnp.int32, (q, q), 1)
    causal = r_io >= c_io
    tri = jnp.where(causal, 1.0, 0.0).astype(BF16)
    cs = sum(jnp.dot(tri, piece, preferred_element_type=F32) for piece in _bf16_pieces(da))
    cs_t = cs.T
    dt_t = dt.T
    e_exp = sum(jnp.dot(piece, expand_ref[...], preferred_element_type=F32) for piece in _bf16_pieces(jnp.exp(cs)))
    w_t = jnp.exp(cs_t[:, q - 1:q] - cs_t) * dt_t

    for g in range(SSM_GROUPS):
        b_g = xc_scr[:, SSM_INNER + g * n:SSM_INNER + (g + 1) * n]
        c_g = xc_scr[:, SSM_INNER + SSM_GROUPS * n + g * n:SSM_INNER + SSM_GROUPS * n + (g + 1) * n]
        c_bf = c_g.astype(BF16)
        cb = _nt_dot(c_bf, b_g.astype(BF16))
        b_t = b_g.T
        yoff_scr[:, g * SSM_GROUP_WIDTH:(g + 1) * SSM_GROUP_WIDTH] = jnp.dot(
            c_bf, ht_scr[g].astype(BF16), preferred_element_type=F32)
        for r in range(SSM_GROUP_HEADS):
            h = g * SSM_GROUP_HEADS + r
            cols = slice(h * SSM_HEAD_DIM, (h + 1) * SSM_HEAD_DIM)
            diff = cs[:, h:h + 1] - cs_t[h:h + 1, :]
            lmat = jnp.exp(jnp.where(causal, diff, -jnp.inf))
            m_h = (cb * lmat * dt_t[h:h + 1, :]).astype(BF16)
            x_h = xc_scr[:, cols].astype(BF16)
            yd_scr[:, cols] = jnp.dot(m_h, x_h, preferred_element_type=F32)
            bts = (b_t * w_t[h:h + 1, :]).astype(BF16)
            st_scr[:, r * SSM_HEAD_DIM:(r + 1) * SSM_HEAD_DIM] = jnp.dot(bts, x_h, preferred_element_type=F32)
        decay = e_exp[q - 1:q, g * SSM_GROUP_WIDTH:(g + 1) * SSM_GROUP_WIDTH]
        ht_scr[g] = ht_scr[g] * decay + st_scr[...]

    xs = xc_scr[0:lc, 0:SSM_INNER]
    y = yd_scr[0:lc, :] + yoff_scr[0:lc, :] * e_exp[0:lc, :] + dfull_ref[...] * xs
    y = y * _silu(z_ref[0])
    for g in range(SSM_GROUPS):
        cols = slice(g * SSM_GROUP_WIDTH, (g + 1) * SSM_GROUP_WIDTH)
        yg = y[:, cols]
        ms = jnp.mean(jnp.square(yg), axis=-1, keepdims=True)
        y_ref[0, :, cols] = (yg * lax.rsqrt(ms + RMS_EPS) * ng_ref[:, cols]).astype(y_ref.dtype)

    @pl.when(c == nchunks - 1)
    def _fin():
        for g in range(SSM_GROUPS):
            hout_ref[0, g] = ht_scr[g].T


def _ssd(src, dt_raw, conv_state, ssm_state, p, *, xbc_col, z_col, out_dtype):
    bsz, seq, _ = src.shape
    lc = min(seq, SSM_CHUNK)
    assert seq % lc == 0 and lc % SUBLANES == 0
    nchunks = seq // lc
    q = SSM_CHUNK
    const = lambda shape: pl.BlockSpec(shape, lambda b, c: (0,) * len(shape))
    y, h_last = pl.pallas_call(
        functools.partial(_ssd_kernel, lc=lc, nchunks=nchunks),
        grid=(bsz, nchunks),
        in_specs=[
            pl.BlockSpec((1, lc, CONV_DIM), lambda b, c: (b, c, xbc_col)),
            pl.BlockSpec((1, lc, SSM_INNER), lambda b, c: (b, c, z_col)),
            pl.BlockSpec((1, lc, LANES), lambda b, c: (b, c, 0)),
            pl.BlockSpec((1, SUBLANES, CONV_DIM), lambda b, c: (b, 0, 0)),
            pl.BlockSpec((1, SSM_GROUPS, SSM_GROUP_WIDTH, SSM_STATE), lambda b, c: (b, 0, 0, 0)),
            const((SSM_CONV, CONV_DIM)), const((1, CONV_DIM)), const((1, LANES)), const((1, LANES)),
            const((1, SSM_INNER)), const((1, SSM_INNER)), const((LANES, SSM_INNER)),
        ],
        out_specs=[
            pl.BlockSpec((1, lc, SSM_INNER), lambda b, c: (b, c, 0)),
            pl.BlockSpec((1, SSM_GROUPS, SSM_GROUP_WIDTH, SSM_STATE), lambda b, c: (b, 0, 0, 0)),
        ],
        out_shape=[jax.ShapeDtypeStruct((bsz, seq, SSM_INNER), out_dtype),
                   jax.ShapeDtypeStruct((bsz, SSM_GROUPS, SSM_GROUP_WIDTH, SSM_STATE), F32)],
        scratch_shapes=[
            pltpu.VMEM((SUBLANES + q, CONV_DIM), F32),
            pltpu.VMEM((q, CONV_DIM), F32),
            pltpu.VMEM((q, LANES), F32),
            pltpu.VMEM((SSM_GROUPS, SSM_STATE, SSM_GROUP_WIDTH), F32),
            pltpu.VMEM((q, SSM_INNER), F32),
            pltpu.VMEM((q, SSM_INNER), F32),
            pltpu.VMEM((SSM_STATE, SSM_GROUP_WIDTH), F32),
        ],
        compiler_params=_cparams("parallel", "arbitrary"),
        name="ssd",
    )(src, src, dt_raw, conv_state, ssm_state, p["conv_w"], p["conv_b"], p["dt_bias"], p["a_log"], p["d_full"],
      p["norm_g"], p["expand"])
    return y, h_last


def _select_blocks(gate, own, n_sel):
    blk = lax.broadcasted_iota(jnp.int32, gate.shape, 1)
    blk_f = blk.astype(F32)
    g = jnp.where(blk < own, gate, -jnp.inf)
    sel = jnp.zeros(gate.shape, F32)
    for t in range(n_sel):
        m = jnp.max(g, axis=1, keepdims=True)
        idx = jnp.min(jnp.where(g == m, blk_f, float(LANES)), axis=1, keepdims=True)
        hit = blk_f == idx
        sel = jnp.where(hit & (own > t), 1.0, sel)
        g = jnp.where(hit, -jnp.inf, g)
    return sel


def _kmean_kernel(k_ref, o_ref):
    rows = k_ref.shape[0]
    o_ref[...] = jnp.sum(k_ref[...].reshape(rows // MOBA_BLOCK, MOBA_BLOCK, k_ref.shape[1]), axis=1) / MOBA_BLOCK


def _block_means(k, *, nblk_pad):
    t = k.shape[0]
    nblk = t // MOBA_BLOCK
    per = _tile(nblk, SUBLANES)
    assert per == SUBLANES or per == nblk
    out = pl.pallas_call(
        _kmean_kernel,
        grid=(nblk // per,),
        in_specs=[pl.BlockSpec((per * MOBA_BLOCK, ATTN_WIDTH), lambda i: (i, 0))],
        out_specs=pl.BlockSpec((per, ATTN_WIDTH), lambda i: (i, 0)),
        out_shape=jax.ShapeDtypeStruct((nblk, ATTN_WIDTH), F32),
        compiler_params=_cparams("parallel"),
        name="moba_block_means",
    )(k)
    return jnp.pad(out, ((0, nblk_pad - nblk), (0, 0)))


def _attn_kernel(qi_ref, kj_ref, qf_ref, k_ref, koh_ref, v_ref, km_ref, o_ref,
                 qa_scr, m_scr, acc_scr, *, tile, sub, heads, n_sel, scale):
    p = pl.program_id(1)
    qi = qi_ref[p]
    kj = kj_ref[p]
    q_pos = qi * tile + lax.broadcasted_iota(jnp.int32, (tile, 1), 0)
    own = lax.shift_right_logical(q_pos, MOBA_BLOCK_SHIFT)
    head_cols = [slice(h * HEAD_DIM, (h + 1) * HEAD_DIM) for h in range(heads)]

    @pl.when(kj == 0)
    def _init():
        m_scr[...] = jnp.full(m_scr.shape, NEG_BIG, F32)
        acc_scr[...] = jnp.zeros(acc_scr.shape, F32)
        for h in range(heads):
            qf = qf_ref[:, head_cols[h]]
            gate = _nt_dot(qf, km_ref[:, head_cols[h]], precision=HIGHEST)
            sel = _select_blocks(gate, own, n_sel)
            blk = lax.broadcasted_iota(jnp.int32, sel.shape, 1)
            qa_scr[h, :, 0:HEAD_DIM] = (qf * (scale * LOG2_E)).astype(BF16)
            qa_scr[h, :, HEAD_DIM:] = jnp.where((sel > 0.5) | (blk == own), 0.0, NEG_BIG).astype(BF16)

    koh = koh_ref[...]
    ones = jnp.ones((tile, HEAD_DIM), BF16)
    k_aug = [jnp.concatenate([k_ref[:, head_cols[h]], koh], axis=1) for h in range(heads)]
    v_aug = [jnp.concatenate([v_ref[:, head_cols[h]], ones], axis=1) for h in range(heads)]

    def update(h, r0, n_keys, diag):
        rows = slice(r0, r0 + sub)
        s = _nt_dot(qa_scr[h, rows, :], k_aug[h][0:n_keys])
        if diag:
            k_pos = kj * tile + lax.broadcasted_iota(jnp.int32, (1, n_keys), 1)
            future = (lax.shift_right_logical(k_pos, MOBA_BLOCK_SHIFT) == own[rows]) & (k_pos > q_pos[rows])
            s = jnp.where(future, NEG_BIG, s)
        m_old = m_scr[h, rows, :]
        m_new = jnp.maximum(m_old, jnp.max(s, axis=1, keepdims=True))
        alpha = jnp.exp2(m_old - m_new)
        pr = jnp.exp2(s - m_new).astype(BF16)
        acc_scr[h, rows, :] = alpha * acc_scr[h, rows, :] + jnp.dot(pr, v_aug[h][0:n_keys],
                                                                    preferred_element_type=F32)
        m_scr[h, rows, :] = m_new

    @pl.when(kj < qi)
    def _past():
        for r0 in range(0, tile, sub):
            for h in range(heads):
                update(h, r0, tile, False)

    @pl.when(kj == qi)
    def _diag():
        for r0 in range(0, tile, sub):
            for h in range(heads):
                update(h, r0, r0 + sub, True)
        for h in range(heads):
            o_ref[:, head_cols[h]] = (acc_scr[h, :, 0:HEAD_DIM] / acc_scr[h, :, HEAD_DIM:]).astype(o_ref.dtype)


def _attend_prompt(q, k, k_bf, v_bf, *, out_dtype):
    t = q.shape[0]
    assert t % MOBA_BLOCK == 0
    nblk = t // MOBA_BLOCK
    assert nblk <= LANES
    n_sel = min(MOBA_TOPK, nblk)
    kmean = _block_means(k, nblk_pad=LANES)
    tile = _tile(t, 1024)
    assert tile % MOBA_BLOCK == 0
    pairs = [(i, j) for i in range(t // tile) for j in range(i + 1)]
    qi_arr = jnp.asarray([a for a, _ in pairs], jnp.int32)
    kj_arr = jnp.asarray([b for _, b in pairs], jnp.int32)
    k_onehot = (jnp.arange(t)[:, None] // MOBA_BLOCK == jnp.arange(LANES)[None, :]).astype(BF16)
    heads = 4
    width = heads * HEAD_DIM
    grid_spec = pltpu.PrefetchScalarGridSpec(
        num_scalar_prefetch=2,
        grid=(N_HEADS // heads, len(pairs)),
        in_specs=[
            pl.BlockSpec((tile, width), lambda h, p, qi, kj: (qi[p], h)),
            pl.BlockSpec((tile, width), lambda h, p, qi, kj: (kj[p], h)),
            pl.BlockSpec((tile, LANES), lambda h, p, qi, kj: (kj[p], 0)),
            pl.BlockSpec((tile, width), lambda h, p, qi, kj: (kj[p], h)),
            pl.BlockSpec((LANES, width), lambda h, p, qi, kj: (0, h)),
        ],
        out_specs=pl.BlockSpec((tile, width), lambda h, p, qi, kj: (qi[p], h)),
        scratch_shapes=[pltpu.VMEM((heads, tile, 2 * HEAD_DIM), BF16),
                        pltpu.VMEM((heads, tile, 1), F32),
                        pltpu.VMEM((heads, tile, 2 * HEAD_DIM), F32)],
    )
    return pl.pallas_call(
        functools.partial(_attn_kernel, tile=tile, sub=MOBA_BLOCK, heads=heads, n_sel=n_sel,
                          scale=HEAD_DIM ** -0.5),
        grid_spec=grid_spec,
        out_shape=jax.ShapeDtypeStruct((t, ATTN_WIDTH), out_dtype),
        compiler_params=_cparams("parallel", "arbitrary"),
        name="moba_prompt_attention",
    )(qi_arr, kj_arr, q, k_bf, k_onehot, v_bf, kmean)


def _page_head(page_ref, h):
    return page_ref[0, pl.ds(h, PAGE_SIZE, stride=N_HEADS), :]


def _paged_attn_kernel(pt_ref, *refs, gp, n_steps, dec, past, n_sel, scale, rider_tiles):
    k_refs = refs[0:gp]
    v_refs = refs[gp:2 * gp]
    q_ref, kn_ref, vn_ref = refs[2 * gp:2 * gp + 3]
    rest = refs[2 * gp + 3:]
    if rider_tiles:
        ra_ref, rb_ref, o_ref, ro_ref = rest[0:4]
        rest = rest[4:]
    else:
        o_ref = rest[0]
        rest = rest[1:]
    qbd_scr, kbf_scr, s_scr, ksum_scr, sel_scr, l_scr, oacc_scr, new_scr = rest
    ph = pl.program_id(1)
    st = pl.program_id(2)

    if rider_tiles:
        @pl.when((pl.program_id(0) * 2 + ph) * n_steps + st < rider_tiles)
        def _rider():
            ro_ref[...] = jnp.dot(ra_ref[...], rb_ref[...], preferred_element_type=F32)

    page = PAGE_SIZE
    step_keys = gp * page
    t_past = past
    cols = N_HEADS * dec
    own = past // MOBA_BLOCK

    @pl.when((ph == 0) & (st == 0))
    def _init():
        new_scr[...] = jnp.zeros(new_scr.shape, F32)
        new_scr[0:dec, :] = q_ref[...]
        for h in range(N_HEADS):
            qt = new_scr[:, h * HEAD_DIM:(h + 1) * HEAD_DIM].T
            if h:
                qt = pltpu.roll(qt, h * dec, 1)
            qbd_scr[h * HEAD_DIM:(h + 1) * HEAD_DIM, :] = qt.astype(BF16)
        ksum_scr[...] = jnp.zeros(ksum_scr.shape, F32)

    @pl.when(ph == 0)
    def _scores():
        pages_per_blk = MOBA_BLOCK // page
        blks_per_step = gp // pages_per_blk
        sums = []
        for g in range(0, gp, pages_per_blk):
            for h in range(N_HEADS):
                tot = None
                for gg in range(g, g + pages_per_blk):
                    kh = _page_head(k_refs[gg], h)
                    kbf_scr[gg * page:(gg + 1) * page, h * HEAD_DIM:(h + 1) * HEAD_DIM] = kh.astype(BF16)
                    part = jnp.sum(kh, axis=0, keepdims=True)
                    tot = part if tot is None else tot + part
                sums.append(tot)
            s_blk = jnp.dot(kbf_scr[g * page:(g + pages_per_blk) * page, :], qbd_scr[...],
                            preferred_element_type=F32) * scale
            s_scr[pl.ds(pl.multiple_of(st * step_keys + g * page, MOBA_BLOCK), MOBA_BLOCK), :] = s_blk
        for s_static in range(n_steps):
            @pl.when(st == s_static)
            def _store_sums(s_static=s_static):
                for j, tot in enumerate(sums):
                    blk = s_static * blks_per_step + j // N_HEADS
                    h = j % N_HEADS
                    ksum_scr[blk:blk + 1, h * HEAD_DIM:(h + 1) * HEAD_DIM] = tot

    @pl.when((ph == 0) & (st == n_steps - 1))
    def _softmax():
        new_scr[...] = jnp.zeros(new_scr.shape, F32)
        new_scr[0:dec, :] = kn_ref[...]
        s_scr[t_past:t_past + page, :] = jnp.dot(new_scr[...].astype(BF16), qbd_scr[...],
                                                 preferred_element_type=F32) * scale
        own_col = jnp.full((dec, 1), own, jnp.int32)
        sel_scr[...] = jnp.zeros(sel_scr.shape, F32)
        for h in range(N_HEADS):
            kmean = ksum_scr[:, h * HEAD_DIM:(h + 1) * HEAD_DIM] / MOBA_BLOCK
            gate = _nt_dot(q_ref[:, h * HEAD_DIM:(h + 1) * HEAD_DIM], kmean, precision=HIGHEST)
            sel_scr[h * dec:(h + 1) * dec, :] = _select_blocks(gate, own_col, n_sel)
        sel_bf = sel_scr[...].astype(BF16)
        col = lax.broadcasted_iota(jnp.int32, (page, LANES), 1)
        krow = lax.broadcasted_iota(jnp.int32, (page, LANES), 0)
        new_ok = (krow <= col % dec) & (col < cols)
        s_new = jnp.where(new_ok, s_scr[t_past:t_past + page, :], NEG_BIG)
        s_scr[t_past:t_past + page, :] = s_new
        blk_col = lax.broadcasted_iota(jnp.int32, (step_keys, LANES), 1)
        key_row = lax.broadcasted_iota(jnp.int32, (step_keys, LANES), 0)

        def mask_body(ci, m):
            rows = pl.ds(pl.multiple_of(ci * step_keys, step_keys), step_keys)
            onehot = (lax.shift_right_logical(ci * step_keys + key_row, MOBA_BLOCK_SHIFT) == blk_col).astype(BF16)
            picked = _nt_dot(onehot, sel_bf)
            sc = jnp.where(picked > 0.5, s_scr[rows, :], NEG_BIG)
            s_scr[rows, :] = sc
            return jnp.maximum(m, jnp.max(sc, axis=0, keepdims=True))

        m = lax.fori_loop(0, n_steps, mask_body, jnp.max(s_new, axis=0, keepdims=True))

        def exp_body(ci, carry):
            rows = pl.ds(pl.multiple_of(ci * step_keys, step_keys), step_keys)
            s_scr[rows, :] = jnp.exp(s_scr[rows, :] - m)
            return carry

        lax.fori_loop(0, n_steps, exp_body, 0)
        s_scr[t_past:t_past + page, :] = jnp.exp(s_new - m)
        l_scr[...] = jnp.zeros(l_scr.shape, F32)
        oacc_scr[...] = jnp.zeros(oacc_scr.shape, F32)

    @pl.when(ph == 1)
    def _values():
        pages_per_blk = MOBA_BLOCK // page
        l_tot = None
        o_tot = None
        for g in range(0, gp, pages_per_blk):
            for gg in range(g, g + pages_per_blk):
                for h in range(N_HEADS):
                    kbf_scr[gg * page:(gg + 1) * page, h * HEAD_DIM:(h + 1) * HEAD_DIM] = _page_head(
                        v_refs[gg], h).astype(BF16)
            rows = pl.ds(pl.multiple_of(st * step_keys + g * page, MOBA_BLOCK), MOBA_BLOCK)
            pr = s_scr[rows, :].T
            l_blk = jnp.sum(pr, axis=1, keepdims=True)
            o_blk = jnp.dot(pr.astype(BF16), kbf_scr[g * page:(g + pages_per_blk) * page, :],
                            preferred_element_type=F32)
            l_tot = l_blk if l_tot is None else l_tot + l_blk
            o_tot = o_blk if o_tot is None else o_tot + o_blk
        l_scr[...] += l_tot
        oacc_scr[...] += o_tot

    @pl.when((ph == 1) & (st == n_steps - 1))
    def _fin():
        new_scr[...] = jnp.zeros(new_scr.shape, F32)
        new_scr[0:dec, :] = vn_ref[...]
        pr = s_scr[t_past:t_past + page, :].T
        l_tot = l_scr[...] + jnp.sum(pr, axis=1, keepdims=True)
        o_full = oacc_scr[...] + jnp.dot(pr.astype(BF16), new_scr[...].astype(BF16), preferred_element_type=F32)
        for h in range(N_HEADS):
            o_ref[:, h * HEAD_DIM:(h + 1) * HEAD_DIM] = (
                o_full[h * dec:(h + 1) * dec, h * HEAD_DIM:(h + 1) * HEAD_DIM] / l_tot[h * dec:(h + 1) * dec, :]
            ).astype(o_ref.dtype)


RIDER_TM, RIDER_TN = 1024, 256


def _attend_paged(cache_k, cache_v, layer, page_table, q, k_new, v_new, *, dec, rider=None):
    bsz, n_pages = page_table.shape
    past = n_pages * PAGE_SIZE
    assert past % MOBA_BLOCK == 0 and dec == SUBLANES and N_HEADS * dec <= LANES
    own = past // MOBA_BLOCK
    assert own + 1 <= LANES
    n_sel = min(MOBA_TOPK, own + 1)
    depth, n_pool = cache_k.shape[0], cache_k.shape[1]
    ck = cache_k.reshape(depth * n_pool, PAGE_SIZE * N_HEADS, HEAD_DIM)
    cv = cache_v.reshape(depth * n_pool, PAGE_SIZE * N_HEADS, HEAD_DIM)
    first = layer * n_pool
    rider_tiles = 0
    if rider is not None:
        ra, rb = rider
        rm, rk = ra.shape
        rn = rb.shape[1]
        rtm, rtn = _tile(rm, RIDER_TM), _tile(rn, RIDER_TN)
        n_col = rn // rtn
        if (rm // rtm) * n_col <= bsz * 2 * (n_pages // _tile(n_pages, 8)):
            rider_tiles = (rm // rtm) * n_col
    gp = _tile(n_pages, 8 if rider_tiles else 16)
    assert gp % (MOBA_BLOCK // PAGE_SIZE) == 0
    n_steps = n_pages // gp
    t_all = past + PAGE_SIZE

    def rider_tile(b, ph, st):
        return jnp.minimum((b * 2 + ph) * n_steps + st, rider_tiles - 1)

    def k_map(g):
        return lambda b, ph, st, pt: (first + pt[b, jnp.where(ph == 0, st, n_steps - 1) * gp + g], 0, 0)

    def v_map(g):
        return lambda b, ph, st, pt: (first + pt[b, jnp.where(ph == 1, st, 0) * gp + g], 0, 0)

    page_spec = lambda fn: pl.BlockSpec((1, PAGE_SIZE * N_HEADS, HEAD_DIM), fn)
    row_spec = pl.BlockSpec((dec, ATTN_WIDTH), lambda b, ph, st, pt: (b, 0))
    in_specs = ([page_spec(k_map(g)) for g in range(gp)] + [page_spec(v_map(g)) for g in range(gp)]
                + [row_spec] * 3)
    out_specs, out_shape, operands = [row_spec], [jax.ShapeDtypeStruct((bsz * dec, ATTN_WIDTH), F32)], []
    if rider_tiles:
        in_specs += [
            pl.BlockSpec((rtm, rk), lambda b, ph, st, pt: (rider_tile(b, ph, st) // n_col, 0)),
            pl.BlockSpec((rk, rtn), lambda b, ph, st, pt: (0, rider_tile(b, ph, st) % n_col)),
        ]
        out_specs.append(pl.BlockSpec(
            (rtm, rtn), lambda b, ph, st, pt: (rider_tile(b, ph, st) // n_col, rider_tile(b, ph, st) % n_col)))
        out_shape.append(jax.ShapeDtypeStruct((rm, rn), F32))
        operands = [ra, rb]
    grid_spec = pltpu.PrefetchScalarGridSpec(
        num_scalar_prefetch=1,
        grid=(bsz, 2, n_steps),
        in_specs=in_specs,
        out_specs=out_specs,
        scratch_shapes=[
            pltpu.VMEM((ATTN_WIDTH, LANES), BF16),
            pltpu.VMEM((gp * PAGE_SIZE, ATTN_WIDTH), BF16),
            pltpu.VMEM((t_all, LANES), F32),
            pltpu.VMEM((LANES, ATTN_WIDTH), F32),
            pltpu.VMEM((LANES, LANES), F32),
            pltpu.VMEM((LANES, 1), F32),
            pltpu.VMEM((LANES, ATTN_WIDTH), F32),
            pltpu.VMEM((PAGE_SIZE, ATTN_WIDTH), F32),
        ],
    )
    outs = pl.pallas_call(
        functools.partial(_paged_attn_kernel, gp=gp, n_steps=n_steps, dec=dec, past=past, n_sel=n_sel,
                          scale=HEAD_DIM ** -0.5, rider_tiles=rider_tiles),
        grid_spec=grid_spec,
        out_shape=out_shape,
        compiler_params=_cparams("arbitrary", "arbitrary", "arbitrary"),
        name="moba_paged_attention",
    )(page_table, *([ck] * gp), *([cv] * gp), q, k_new, v_new, *operands)
    if rider_tiles:
        return outs[0], outs[1]
    return outs[0], (None if rider is None else _project(ra, rb, tn=1024, name="in_proj_rest"))


def _layer_norm(x, g, b):
    mu = jnp.mean(x, axis=-1, keepdims=True)
    xc = x - mu
    var = jnp.mean(jnp.square(xc), axis=-1, keepdims=True)
    return xc * lax.rsqrt(var + LN_EPS) * g + b


def _merge_ln_kernel(oa_ref, ys_ref, ga_ref, gs_ref, res_ref, wa_ref, ws_ref, wo_ref, g_ref, b_ref, o_ref, obf_ref,
                     *, alpha, sub):
    for r0 in range(0, oa_ref.shape[0], sub):
        rows = slice(r0, r0 + sub)
        pa = jnp.dot(oa_ref[rows, :].astype(BF16), wa_ref[...], preferred_element_type=F32)
        ps = jnp.dot(ys_ref[rows, :].astype(BF16), ws_ref[...], preferred_element_type=F32)
        merged = jax.nn.sigmoid(ga_ref[rows, :]) * pa + jax.nn.sigmoid(gs_ref[rows, :]) * ps
        acc = jnp.dot(merged.astype(BF16), wo_ref[...], preferred_element_type=F32)
        y = _layer_norm(alpha * res_ref[rows, :] + acc, g_ref[...], b_ref[...])
        o_ref[rows, :] = y
        obf_ref[rows, :] = y.astype(BF16)


def _merge_ln(o_attn, y_s, rest, res, w_pa, w_ps, w_out, g, b, *, ga_col, gs_col, alpha, tm=256):
    m = o_attn.shape[0]
    tm = _tile(m, tm)
    row = lambda width, col=0: pl.BlockSpec((tm, width), lambda i: (i, col))
    resident = lambda shape: pl.BlockSpec(shape, lambda i: (0, 0), pipeline_mode=pl.Buffered(1))
    return pl.pallas_call(
        functools.partial(_merge_ln_kernel, alpha=alpha, sub=_tile(tm, 128)),
        grid=(m // tm,),
        in_specs=[row(ATTN_WIDTH), row(SSM_INNER), row(D_MODEL, ga_col), row(D_MODEL, gs_col), row(D_MODEL),
                  resident((ATTN_WIDTH, D_MODEL)), resident((SSM_INNER, D_MODEL)), resident((D_MODEL, D_MODEL)),
                  resident((1, D_MODEL)), resident((1, D_MODEL))],
        out_specs=[row(D_MODEL), row(D_MODEL)],
        out_shape=[jax.ShapeDtypeStruct((m, D_MODEL), F32), jax.ShapeDtypeStruct((m, D_MODEL), BF16)],
        compiler_params=_cparams("parallel"),
        name="merge_out_proj_layernorm",
    )(o_attn, y_s, rest, rest, res, w_pa, w_ps, w_out, g, b)


def _gelu(x):
    return 0.5 * x * (1.0 + lax.erf(x * np.float32(np.sqrt(0.5))))


def _glu_long_kernel(h_ref, wg_ref, wu_ref, st_ref, cw_ref, cb_ref, act_ref, tail_ref, ext_scr, *, tm, n_row_tiles):
    i = pl.program_id(1)
    halo = SUBLANES

    @pl.when(i == 0)
    def _init():
        ext_scr[0:halo, :] = st_ref[...]

    sub = _tile(tm, 256)
    for r0 in range(0, tm, sub):
        hh = h_ref[r0:r0 + sub, :]
        g_up = jnp.dot(hh, wg_ref[...], preferred_element_type=F32)
        u = jnp.dot(hh, wu_ref[...], preferred_element_type=F32)
        ext_scr[halo + r0:halo + r0 + sub, :] = g_up
        first = halo - (FFN_CONV - 1) + r0
        acc = ext_scr[first:first + sub, :] * cw_ref[0:1, :]
        for kk in range(1, FFN_CONV):
            acc = acc + ext_scr[first + kk:first + kk + sub, :] * cw_ref[kk:kk + 1, :]
        act_ref[r0:r0 + sub, :] = (_gelu(acc + cb_ref[...]) * u).astype(act_ref.dtype)
    ext_scr[0:halo, :] = ext_scr[tm:tm + halo, :]

    @pl.when(i == n_row_tiles - 1)
    def _tail():
        tail_ref[...] = ext_scr[tm:tm + halo, :]


def _glu_long(h_bf, w_gate, w_up, state, conv_w, conv_b, *, tm=1024, tn=512):
    m, k = h_bf.shape
    tm = _tile(m, tm)
    n_row_tiles = m // tm
    return pl.pallas_call(
        functools.partial(_glu_long_kernel, tm=tm, n_row_tiles=n_row_tiles),
        grid=(D_FF // tn, n_row_tiles),
        in_specs=[pl.BlockSpec((tm, k), lambda j, i: (i, 0)), pl.BlockSpec((k, tn), lambda j, i: (0, j)),
                  pl.BlockSpec((k, tn), lambda j, i: (0, j)), pl.BlockSpec((SUBLANES, tn), lambda j, i: (0, j)),
                  pl.BlockSpec((FFN_CONV, tn), lambda j, i: (0, j)), pl.BlockSpec((1, tn), lambda j, i: (0, j))],
        out_specs=[pl.BlockSpec((tm, tn), lambda j, i: (i, j)), pl.BlockSpec((SUBLANES, tn), lambda j, i: (0, j))],
        out_shape=[jax.ShapeDtypeStruct((m, D_FF), BF16), jax.ShapeDtypeStruct((SUBLANES, D_FF), F32)],
        scratch_shapes=[pltpu.VMEM((SUBLANES + tm, tn), F32)],
        compiler_params=_cparams("parallel", "arbitrary"),
        name="convglu_up_long",
    )(h_bf, w_gate, w_up, state, conv_w, conv_b)


def _glu_short_kernel(h_ref, wg_ref, wu_ref, p1_ref, p2_ref, cw_ref, cb_ref, act_ref, gup_ref, *, seq):
    hh = h_ref[...]
    g_up = jnp.dot(hh, wg_ref[...], preferred_element_type=F32)
    u = jnp.dot(hh, wu_ref[...], preferred_element_type=F32)
    gup_ref[...] = g_up
    r = lax.broadcasted_iota(jnp.int32, g_up.shape, 0) % seq
    back1 = jnp.where(r >= 1, pltpu.roll(g_up, 1, 0), p1_ref[...])
    back2 = jnp.where(r >= 2, pltpu.roll(g_up, 2, 0), p2_ref[...])
    acc = back2 * cw_ref[0:1, :] + back1 * cw_ref[1:2, :] + g_up * cw_ref[2:3, :]
    act_ref[...] = (_gelu(acc + cb_ref[...]) * u).astype(act_ref.dtype)


def _glu_short(h_bf, w_gate, w_up, p1, p2, conv_w, conv_b, *, seq, tn=512):
    m, k = h_bf.shape
    full = lambda j: (0, j)
    return pl.pallas_call(
        functools.partial(_glu_short_kernel, seq=seq),
        grid=(D_FF // tn,),
        in_specs=[pl.BlockSpec((m, k), lambda j: (0, 0)), pl.BlockSpec((k, tn), full), pl.BlockSpec((k, tn), full),
                  pl.BlockSpec((m, tn), full), pl.BlockSpec((m, tn), full),
                  pl.BlockSpec((FFN_CONV, tn), full), pl.BlockSpec((1, tn), full)],
        out_specs=[pl.BlockSpec((m, tn), full), pl.BlockSpec((m, tn), full)],
        out_shape=[jax.ShapeDtypeStruct((m, D_FF), BF16), jax.ShapeDtypeStruct((m, D_FF), F32)],
        compiler_params=_cparams("parallel"),
        name="convglu_up_short",
    )(h_bf, w_gate, w_up, p1, p2, conv_w, conv_b)


def _down_ln_kernel(a_ref, w_ref, res_ref, g_ref, b_ref, o_ref, *, alpha, sub):
    for r0 in range(0, a_ref.shape[0], sub):
        rows = slice(r0, r0 + sub)
        acc = jnp.dot(a_ref[rows, :], w_ref[...], preferred_element_type=F32)
        o_ref[rows, :] = _layer_norm(alpha * res_ref[rows, :] + acc, g_ref[...], b_ref[...])


def _down_ln(a, w, res, g, b, *, alpha, tm=256):
    m, k = a.shape
    n = w.shape[1]
    tm = _tile(m, tm)
    resident = lambda shape: pl.BlockSpec(shape, lambda i: (0, 0), pipeline_mode=pl.Buffered(1))
    return pl.pallas_call(
        functools.partial(_down_ln_kernel, alpha=alpha, sub=_tile(tm, 128)),
        grid=(m // tm,),
        in_specs=[pl.BlockSpec((tm, k), lambda i: (i, 0)), resident((k, n)), pl.BlockSpec((tm, n), lambda i: (i, 0)),
                  resident((1, n)), resident((1, n))],
        out_specs=pl.BlockSpec((tm, n), lambda i: (i, 0)),
        out_shape=jax.ShapeDtypeStruct((m, n), F32),
        compiler_params=_cparams("parallel"),
        name="down_proj_layernorm",
    )(a, w, res, g, b)


REST_Z, REST_GA, REST_GS = 0, 1, 2
REST_XBC = 3 * D_MODEL // CONV_DIM
assert REST_XBC * CONV_DIM == 3 * D_MODEL


def _prep_layer(w_in, ssm_conv_w, ssm_conv_b, dt_bias, a_log, d_skip, ssm_norm_g, w_proj_attn, w_proj_ssm, w_out,
                ln1_g, ln1_b, w_gate, w_up, ffn_conv_w, ffn_conv_b, w_down, ln2_g, ln2_b):
    widths = (ATTN_WIDTH, ATTN_WIDTH, ATTN_WIDTH, SSM_INNER, CONV_DIM, SSM_HEADS, D_MODEL, D_MODEL)
    offs = np.concatenate([[0], np.cumsum(widths)])
    wq, wk, wv, wz, wxbc, wdt, wga, wgs = (w_in[:, offs[i]:offs[i + 1]] for i in range(8))
    row = lambda v: v.reshape(1, -1).astype(F32)
    pad_lanes = lambda v: jnp.pad(row(v), ((0, 0), (0, LANES - v.shape[-1])))
    return {
        "w_q": wq.astype(BF16), "w_k": wk.astype(BF16), "w_v": wv.astype(BF16),
        "w_rest": jnp.concatenate([wz, wga, wgs, wxbc], axis=1).astype(BF16),
        "w_dt": jnp.pad(wdt, ((0, 0), (0, LANES - SSM_HEADS))).astype(BF16),
        "conv_w": ssm_conv_w.astype(F32), "conv_b": row(ssm_conv_b),
        "dt_bias": pad_lanes(dt_bias), "a_log": pad_lanes(a_log),
        "d_full": row(jnp.repeat(d_skip, SSM_HEAD_DIM)), "norm_g": row(ssm_norm_g),
        "expand": (jnp.arange(LANES)[:, None] == jnp.arange(SSM_INNER)[None, :] // SSM_HEAD_DIM).astype(BF16),
        "w_pa": w_proj_attn.astype(BF16), "w_ps": w_proj_ssm.astype(BF16), "w_out": w_out.astype(BF16),
        "ln1_g": row(ln1_g), "ln1_b": row(ln1_b),
        "w_gate": w_gate.astype(BF16), "w_up": w_up.astype(BF16),
        "ffn_conv_w": ffn_conv_w.astype(F32), "ffn_conv_b": row(ffn_conv_b),
        "w_down": w_down.astype(BF16), "ln2_g": row(ln2_g), "ln2_b": row(ln2_b),
    }


def _rope_tables(pos):
    half = HEAD_DIM // 2
    inv = ROPE_THETA ** (-jnp.arange(half, dtype=F32) * 2.0 / HEAD_DIM)
    ang = pos.astype(F32)[:, None] * inv[None, :]
    cos, sin = jnp.cos(ang), jnp.sin(ang)
    return jnp.concatenate([cos, cos], axis=1), jnp.concatenate([-sin, sin], axis=1)


def _layer(x, pos, p, *, conv_state, ssm_state, ffn_state, paged, alpha, rest=None, rider=None):
    bsz, seq, _ = x.shape
    m = bsz * seq
    x2 = x.reshape(m, D_MODEL)
    x_bf = x2.astype(BF16)
    cos, sin = _rope_tables(pos)
    if bsz > 1:
        cos, sin = jnp.tile(cos, (bsz, 1)), jnp.tile(sin, (bsz, 1))
    prompt = paged is None
    q = _project(x_bf, p["w_q"], (cos, sin), name="in_proj_q_rope")
    k_new = _project(x_bf, p["w_k"], (cos, sin), with_bf16=prompt, name="in_proj_k_rope")
    v_new = _project(x_bf, p["w_v"], with_bf16=prompt, name="in_proj_v")
    if prompt:
        (k_new, k_bf), (v_new, v_bf) = k_new, v_new
    if rest is None:
        rest = _project(x_bf, p["w_rest"], tn=1024, name="in_proj_rest")
    dt_raw = _project(x_bf, p["w_dt"], tn=LANES, name="in_proj_dt")

    assert seq >= SSM_CONV - 1
    new_conv = rest.reshape(bsz, seq, -1)[:, seq - (SSM_CONV - 1):, REST_XBC * CONV_DIM:(REST_XBC + 1) * CONV_DIM]

    if prompt:
        assert bsz == 1
        o_attn = _attend_prompt(q, k_new, k_bf, v_bf, out_dtype=BF16)
    else:
        cache_k, cache_v, layer, page_table = paged
        o_attn, rider_out = _attend_paged(cache_k, cache_v, layer, page_table, q, k_new, v_new, dec=seq, rider=rider)

    cst = jnp.pad(conv_state.astype(F32), ((0, 0), (SUBLANES - (SSM_CONV - 1), 0), (0, 0)))
    h0 = ssm_state.astype(F32).reshape(bsz, SSM_GROUPS, SSM_GROUP_WIDTH, SSM_STATE)
    y_s, h_last = _ssd(rest.reshape(bsz, seq, -1), dt_raw.reshape(bsz, seq, LANES), cst, h0, p,
                       xbc_col=REST_XBC, z_col=REST_Z, out_dtype=BF16 if seq % 16 == 0 else F32)
    h_new = h_last.reshape(bsz, SSM_HEADS, SSM_HEAD_DIM, SSM_STATE)

    h1, h1_bf = _merge_ln(o_attn, y_s.reshape(m, SSM_INNER), rest, x2, p["w_pa"], p["w_ps"], p["w_out"],
                          p["ln1_g"], p["ln1_b"], ga_col=REST_GA, gs_col=REST_GS, alpha=alpha)

    if bsz == 1:
        st = jnp.pad(ffn_state[0].astype(F32), ((SUBLANES - (FFN_CONV - 1), 0), (0, 0)))
        act, tail = _glu_long(h1_bf, p["w_gate"], p["w_up"], st, p["ffn_conv_w"], p["ffn_conv_b"])
        assert seq >= FFN_CONV - 1
        new_ffn = tail[None, SUBLANES - (FFN_CONV - 1):]
    else:
        assert seq == SUBLANES
        fs = ffn_state.astype(F32)
        p1 = jnp.pad(fs[:, 1:2], ((0, 0), (0, seq - 1), (0, 0))).reshape(m, D_FF)
        p2 = jnp.pad(fs, ((0, 0), (0, seq - 2), (0, 0))).reshape(m, D_FF)
        act, g_up = _glu_short(h1_bf, p["w_gate"], p["w_up"], p1, p2, p["ffn_conv_w"], p["ffn_conv_b"], seq=seq)
        new_ffn = g_up.reshape(bsz, seq, D_FF)[:, seq - (FFN_CONV - 1):]
    y = _down_ln(act, p["w_down"], h1, p["ln2_g"], p["ln2_b"], alpha=alpha)
    outs = (y.reshape(bsz, seq, D_MODEL), k_new.reshape(bsz, seq, N_HEADS, HEAD_DIM),
            v_new.reshape(bsz, seq, N_HEADS, HEAD_DIM), h_new, new_conv, new_ffn)
    return outs if prompt else outs + (rider_out,)


def kernel(x_prompt, x_sample, cache_k, cache_v, state_ssm, state_conv, state_ffn_conv, page_table, w_in, ssm_conv_w, ssm_conv_b, dt_bias, a_log, d_skip, ssm_norm_g, w_proj_attn, w_proj_ssm, w_out, ln1_g, ln1_b, w_gate, w_up, ffn_conv_w, ffn_conv_b, w_down, ln2_g, ln2_b):
    depth = w_in.shape[0]
    alpha = (2.0 * depth) ** 0.25
    bp, lp = x_prompt.shape[0], x_prompt.shape[1]
    past_len = page_table.shape[1] * PAGE_SIZE
    pos_p = jnp.arange(lp, dtype=jnp.int32)
    pos_s = past_len + jnp.arange(x_sample.shape[1], dtype=jnp.int32)
    hp, hs = x_prompt, x_sample
    outs_p, outs_s = [], []
    for l in range(depth):
        p = _prep_layer(w_in[l], ssm_conv_w[l], ssm_conv_b[l], dt_bias[l], a_log[l], d_skip[l], ssm_norm_g[l],
                        w_proj_attn[l], w_proj_ssm[l], w_out[l], ln1_g[l], ln1_b[l], w_gate[l], w_up[l],
                        ffn_conv_w[l], ffn_conv_b[l], w_down[l], ln2_g[l], ln2_b[l])
        conv0 = jnp.zeros((bp, SSM_CONV - 1, CONV_DIM), F32)
        ssm0 = jnp.zeros((bp, SSM_HEADS, SSM_HEAD_DIM, SSM_STATE), F32)
        ffn0 = jnp.zeros((bp, FFN_CONV - 1, D_FF), F32)
        xp_bf = hp.reshape(bp * lp, D_MODEL).astype(BF16)
        hs, *rest_s, proj_p = _layer(hs, pos_s, p, conv_state=state_conv[l], ssm_state=state_ssm[l],
                                     ffn_state=state_ffn_conv[l], paged=(cache_k, cache_v, l, page_table),
                                     alpha=alpha, rider=(xp_bf, p["w_rest"]))
        hp, *rest_p = _layer(hp, pos_p, p, conv_state=conv0, ssm_state=ssm0, ffn_state=ffn0, paged=None, alpha=alpha,
                             rest=proj_p)
        outs_p.append(rest_p)
        outs_s.append(rest_s)
    stack = lambda outs, i: jnp.stack([o[i] for o in outs])
    return (hp, hs, *(stack(outs_p, i) for i in range(5)), *(stack(outs_s, i) for i in range(5)))
```

```python
import functools

import jax
import jax.numpy as jnp
import numpy as np
from jax import lax
from jax.experimental import pallas as pl
from jax.experimental.pallas import tpu as pltpu

F32 = jnp.float32
BF16 = jnp.bfloat16
HIGHEST = lax.Precision.HIGHEST

D_MODEL = 2048
PAGE_SIZE = 128
N_HEADS = 8
HEAD_DIM = 128
ATTN_WIDTH = N_HEADS * HEAD_DIM
MOBA_BLOCK = 256
MOBA_BLOCK_SHIFT = 8
MOBA_TOPK = 3
ROPE_THETA = 10000.0
SSM_INNER = D_MODEL
SSM_HEAD_DIM = 64
SSM_HEADS = SSM_INNER // SSM_HEAD_DIM
SSM_GROUPS = 4
SSM_GROUP_HEADS = SSM_HEADS // SSM_GROUPS
SSM_GROUP_WIDTH = SSM_INNER // SSM_GROUPS
SSM_STATE = 128
SSM_CONV = 4
SSM_CHUNK = 128
CONV_DIM = SSM_INNER + 2 * SSM_GROUPS * SSM_STATE
D_FF = ((8 * D_MODEL // 3 + 255) // 256) * 256
FFN_CONV = 3
LN_EPS = 1e-5
RMS_EPS = 1e-5

LANES = 128
SUBLANES = 8
VMEM_LIMIT_BYTES = 56 * 1024 * 1024

NEG_BIG = -1e30
LOG2_E = 1.4426950408889634


def _cparams(*sem):
    return pltpu.CompilerParams(dimension_semantics=sem, vmem_limit_bytes=VMEM_LIMIT_BYTES)


def _tile(n, pref):
    t = min(pref, n)
    while n % t:
        t //= 2
    return t


def _nt_dot(a, b, precision=None):
    return lax.dot_general(a, b, (((1,), (1,)), ((), ())), precision=precision, preferred_element_type=F32)


def _proj_kernel(*refs, rope, with_bf16):
    a_ref, b_ref = refs[0:2]
    outs = refs[4:] if rope else refs[2:]
    acc = jnp.dot(a_ref[...], b_ref[...], preferred_element_type=F32)
    if rope:
        cos = refs[2][...]
        sin = refs[3][...]
    for h in range(acc.shape[1] // HEAD_DIM):
        cols = slice(h * HEAD_DIM, (h + 1) * HEAD_DIM)
        val = acc[:, cols]
        if rope:
            val = val * cos + pltpu.roll(val, HEAD_DIM // 2, 1) * sin
        outs[0][:, cols] = val
        if with_bf16:
            outs[1][:, cols] = val.astype(BF16)


def _project(a, b, rope=None, *, with_bf16=False, tm=1024, tn=512, name):
    m, k = a.shape
    n = b.shape[1]
    tm, tn = _tile(m, tm), _tile(n, tn)
    assert tn % HEAD_DIM == 0
    in_specs = [pl.BlockSpec((tm, k), lambda i, j: (i, 0)), pl.BlockSpec((k, tn), lambda i, j: (0, j))]
    if rope is not None:
        in_specs += [pl.BlockSpec((tm, HEAD_DIM), lambda i, j: (i, 0))] * 2
    out_spec = pl.BlockSpec((tm, tn), lambda i, j: (i, j))
    out = pl.pallas_call(
        functools.partial(_proj_kernel, rope=rope is not None, with_bf16=with_bf16),
        grid=(m // tm, n // tn),
        in_specs=in_specs,
        out_specs=[out_spec] * (2 if with_bf16 else 1),
        out_shape=[jax.ShapeDtypeStruct((m, n), F32)] + ([jax.ShapeDtypeStruct((m, n), BF16)] if with_bf16 else []),
        compiler_params=_cparams("parallel", "arbitrary"),
        name=name,
    )(a, b, *(rope or ()))
    return out if with_bf16 else out[0]


def _softplus(x):
    return jnp.maximum(x, 0.0) + jnp.log1p(jnp.exp(-jnp.abs(x)))


def _bf16_pieces(x):
    hi = x.astype(BF16)
    rest = x - hi.astype(F32)
    mid = rest.astype(BF16)
    return hi, mid, (rest - mid.astype(F32)).astype(BF16)


def _silu(x):
    return x * (0.5 * jnp.tanh(0.5 * x) + 0.5)


def _ssd_kernel(xbc_ref, z_ref, dt_ref, cst_ref, h0_ref, cw_ref, cb_ref, dtb_ref, alog_ref, dfull_ref, ng_ref, expand_ref,
                y_ref, hout_ref,
                ext_scr, xc_scr, dt_scr, ht_scr, yd_scr, yoff_scr, st_scr, *, lc, nchunks):
    q = SSM_CHUNK
    n = SSM_STATE
    c = pl.program_id(1)
    halo = SUBLANES

    @pl.when(c == 0)
    def _init():
        if lc < q:
            ext_scr[...] = jnp.zeros(ext_scr.shape, F32)
            dt_scr[...] = jnp.zeros(dt_scr.shape, F32)
        ext_scr[0:halo, :] = cst_ref[0]
        for g in range(SSM_GROUPS):
            ht_scr[g] = h0_ref[0, g].T

    ext_scr[halo:halo + lc, :] = xbc_ref[0]

    slab = 512
    row = lax.broadcasted_iota(jnp.int32, (q, slab), 0)
    for s in range(CONV_DIM // slab):
        cols = slice(s * slab, (s + 1) * slab)
        acc = ext_scr[halo - 3:halo - 3 + q, cols] * cw_ref[0:1, cols]
        for kk in range(1, SSM_CONV):
            acc = acc + ext_scr[halo - 3 + kk:halo - 3 + kk + q, cols] * cw_ref[kk:kk + 1, cols]
        act = _silu(acc + cb_ref[:, cols])
        if lc < q:
            act = jnp.where(row < lc, act, 0.0)
        xc_scr[:, cols] = act
    if nchunks > 1:
        ext_scr[0:halo, :] = ext_scr[lc:lc + halo, :]

    dt_new = _softplus(dt_ref[0] + dtb_ref[...])
    if lc < q:
        dt_scr[0:lc, :] = dt_new
        dt = dt_scr[...]
    else:
        dt = dt_new
    a = -jnp.exp(alog_ref[...])
    da = dt * a
    r_io = lax.broadcasted_iota(jnp.int32, (q, q), 0)
    c_io = lax.broadcasted_iota(jnp.int32, (q, q), 1)
    causal = r_io >= c_io
    tri = jnp.where(causal, 1.0, 0.0).astype(BF16)
    cs = sum(jnp.dot(tri, piece, preferred_element_type=F32) for piece in _bf16_pieces(da))
    cs_t = cs.T
    dt_t = dt.T
    e_exp = sum(jnp.dot(piece, expand_ref[...], preferred_element_type=F32) for piece in _bf16_pieces(jnp.exp(cs)))
    w_t = jnp.exp(cs_t[:, q - 1:q] - cs_t) * dt_t

    for g in range(SSM_GROUPS):
        b_g = xc_scr[:, SSM_INNER + g * n:SSM_INNER + (g + 1) * n]
        c_g = xc_scr[:, SSM_INNER + SSM_GROUPS * n + g * n:SSM_INNER + SSM_GROUPS * n + (g + 1) * n]
        c_bf = c_g.astype(BF16)
        cb = _nt_dot(c_bf, b_g.astype(BF16))
        b_t = b_g.T
        yoff_scr[:, g * SSM_GROUP_WIDTH:(g + 1) * SSM_GROUP_WIDTH] = jnp.dot(
            c_bf, ht_scr[g].astype(BF16), preferred_element_type=F32)
        for r in range(SSM_GROUP_HEADS):
            h = g * SSM_GROUP_HEADS + r
            cols = slice(h * SSM_HEAD_DIM, (h + 1) * SSM_HEAD_DIM)
            diff = cs[:, h:h + 1] - cs_t[h:h + 1, :]
            lmat = jnp.exp(jnp.where(causal, diff, -jnp.inf))
            m_h = (cb * lmat * dt_t[h:h + 1, :]).astype(BF16)
            x_h = xc_scr[:, cols].astype(BF16)
            yd_scr[:, cols] = jnp.dot(m_h, x_h, preferred_element_type=F32)
            bts = (b_t * w_t[h:h + 1, :]).astype(BF16)
            st_scr[:, r * SSM_HEAD_DIM:(r + 1) * SSM_HEAD_DIM] = jnp.dot(bts, x_h, preferred_element_type=F32)
        decay = e_exp[q - 1:q, g * SSM_GROUP_WIDTH:(g + 1) * SSM_GROUP_WIDTH]
        ht_scr[g] = ht_scr[g] * decay + st_scr[...]

    xs = xc_scr[0:lc, 0:SSM_INNER]
    y = yd_scr[0:lc, :] + yoff_scr[0:lc, :] * e_exp[0:lc, :] + dfull_ref[...] * xs
    y = y * _silu(z_ref[0])
    for g in range(SSM_GROUPS):
        cols = slice(g * SSM_GROUP_WIDTH, (g + 1) * SSM_GROUP_WIDTH)
        yg = y[:, cols]
        ms = jnp.mean(jnp.square(yg), axis=-1, keepdims=True)
        y_ref[0, :, cols] = (yg * lax.rsqrt(ms + RMS_EPS) * ng_ref[:, cols]).astype(y_ref.dtype)

    @pl.when(c == nchunks - 1)
    def _fin():
        for g in range(SSM_GROUPS):
            hout_ref[0, g] = ht_scr[g].T


def _ssd(src, dt_raw, conv_state, ssm_state, p, *, xbc_col, z_col, out_dtype):
    bsz, seq, _ = src.shape
    lc = min(seq, SSM_CHUNK)
    assert seq % lc == 0 and lc % SUBLANES == 0
    nchunks = seq // lc
    q = SSM_CHUNK
    const = lambda shape: pl.BlockSpec(shape, lambda b, c: (0,) * len(shape))
    y, h_last = pl.pallas_call(
        functools.partial(_ssd_kernel, lc=lc, nchunks=nchunks),
        grid=(bsz, nchunks),
        in_specs=[
            pl.BlockSpec((1, lc, CONV_DIM), lambda b, c: (b, c, xbc_col)),
            pl.BlockSpec((1, lc, SSM_INNER), lambda b, c: (b, c, z_col)),
            pl.BlockSpec((1, lc, LANES), lambda b, c: (b, c, 0)),
            pl.BlockSpec((1, SUBLANES, CONV_DIM), lambda b, c: (b, 0, 0)),
            pl.BlockSpec((1, SSM_GROUPS, SSM_GROUP_WIDTH, SSM_STATE), lambda b, c: (b, 0, 0, 0)),
            const((SSM_CONV, CONV_DIM)), const((1, CONV_DIM)), const((1, LANES)), const((1, LANES)),
            const((1, SSM_INNER)), const((1, SSM_INNER)), const((LANES, SSM_INNER)),
        ],
        out_specs=[
            pl.BlockSpec((1, lc, SSM_INNER), lambda b, c: (b, c, 0)),
            pl.BlockSpec((1, SSM_GROUPS, SSM_GROUP_WIDTH, SSM_STATE), lambda b, c: (b, 0, 0, 0)),
        ],
        out_shape=[jax.ShapeDtypeStruct((bsz, seq, SSM_INNER), out_dtype),
                   jax.ShapeDtypeStruct((bsz, SSM_GROUPS, SSM_GROUP_WIDTH, SSM_STATE), F32)],
        scratch_shapes=[
            pltpu.VMEM((SUBLANES + q, CONV_DIM), F32),
            pltpu.VMEM((q, CONV_DIM), F32),
            pltpu.VMEM((q, LANES), F32),
            pltpu.VMEM((SSM_GROUPS, SSM_STATE, SSM_GROUP_WIDTH), F32),
            pltpu.VMEM((q, SSM_INNER), F32),
            pltpu.VMEM((q, SSM_INNER), F32),
            pltpu.VMEM((SSM_STATE, SSM_GROUP_WIDTH), F32),
        ],
        compiler_params=_cparams("parallel", "arbitrary"),
        name="ssd",
    )(src, src, dt_raw, conv_state, ssm_state, p["conv_w"], p["conv_b"], p["dt_bias"], p["a_log"], p["d_full"],
      p["norm_g"], p["expand"])
    return y, h_last


def _select_blocks(gate, own, n_sel):
    blk = lax.broadcasted_iota(jnp.int32, gate.shape, 1)
    blk_f = blk.astype(F32)
    g = jnp.where(blk < own, gate, -jnp.inf)
    sel = jnp.zeros(gate.shape, F32)
    for t in range(n_sel):
        m = jnp.max(g, axis=1, keepdims=True)
        idx = jnp.min(jnp.where(g == m, blk_f, float(LANES)), axis=1, keepdims=True)
        hit = blk_f == idx
        sel = jnp.where(hit & (own > t), 1.0, sel)
        g = jnp.where(hit, -jnp.inf, g)
    return sel


def _kmean_kernel(k_ref, o_ref):
    rows = k_ref.shape[0]
    o_ref[...] = jnp.sum(k_ref[...].reshape(rows // MOBA_BLOCK, MOBA_BLOCK, k_ref.shape[1]), axis=1) / MOBA_BLOCK


def _block_means(k, *, nblk_pad):
    t = k.shape[0]
    nblk = t // MOBA_BLOCK
    per = _tile(nblk, SUBLANES)
    assert per == SUBLANES or per == nblk
    out = pl.pallas_call(
        _kmean_kernel,
        grid=(nblk // per,),
        in_specs=[pl.BlockSpec((per * MOBA_BLOCK, ATTN_WIDTH), lambda i: (i, 0))],
        out_specs=pl.BlockSpec((per, ATTN_WIDTH), lambda i: (i, 0)),
        out_shape=jax.ShapeDtypeStruct((nblk, ATTN_WIDTH), F32),
        compiler_params=_cparams("parallel"),
        name="moba_block_means",
    )(k)
    return jnp.pad(out, ((0, nblk_pad - nblk), (0, 0)))


def _attn_kernel(qi_ref, kj_ref, qf_ref, k_ref, koh_ref, v_ref, km_ref, o_ref,
                 qa_scr, m_scr, acc_scr, *, tile, sub, heads, n_sel, scale):
    p = pl.program_id(1)
    qi = qi_ref[p]
    kj = kj_ref[p]
    q_pos = qi * tile + lax.broadcasted_iota(jnp.int32, (tile, 1), 0)
    own = lax.shift_right_logical(q_pos, MOBA_BLOCK_SHIFT)
    head_cols = [slice(h * HEAD_DIM, (h + 1) * HEAD_DIM) for h in range(heads)]

    @pl.when(kj == 0)
    def _init():
        m_scr[...] = jnp.full(m_scr.shape, NEG_BIG, F32)
        acc_scr[...] = jnp.zeros(acc_scr.shape, F32)
        for h in range(heads):
            qf = qf_ref[:, head_cols[h]]
            gate = _nt_dot(qf, km_ref[:, head_cols[h]], precision=HIGHEST)
            sel = _select_blocks(gate, own, n_sel)
            blk = lax.broadcasted_iota(jnp.int32, sel.shape, 1)
            qa_scr[h, :, 0:HEAD_DIM] = (qf * (scale * LOG2_E)).astype(BF16)
            qa_scr[h, :, HEAD_DIM:] = jnp.where((sel > 0.5) | (blk == own), 0.0, NEG_BIG).astype(BF16)

    koh = koh_ref[...]
    ones = jnp.ones((tile, HEAD_DIM), BF16)
    k_aug = [jnp.concatenate([k_ref[:, head_cols[h]], koh], axis=1) for h in range(heads)]
    v_aug = [jnp.concatenate([v_ref[:, head_cols[h]], ones], axis=1) for h in range(heads)]

    def update(h, r0, n_keys, diag):
        rows = slice(r0, r0 + sub)
        s = _nt_dot(qa_scr[h, rows, :], k_aug[h][0:n_keys])
        if diag:
            k_pos = kj * tile + lax.broadcasted_iota(jnp.int32, (1, n_keys), 1)
            future = (lax.shift_right_logical(k_pos, MOBA_BLOCK_SHIFT) == own[rows]) & (k_pos > q_pos[rows])
            s = jnp.where(future, NEG_BIG, s)
        m_old = m_scr[h, rows, :]
        m_new = jnp.maximum(m_old, jnp.max(s, axis=1, keepdims=True))
        alpha = jnp.exp2(m_old - m_new)
        pr = jnp.exp2(s - m_new).astype(BF16)
        acc_scr[h, rows, :] = alpha * acc_scr[h, rows, :] + jnp.dot(pr, v_aug[h][0:n_keys],
                                                                    preferred_element_type=F32)
        m_scr[h, rows, :] = m_new

    @pl.when(kj < qi)
    def _past():
        for r0 in range(0, tile, sub):
            for h in range(heads):
                update(h, r0, tile, False)

    @pl.when(kj == qi)
    def _diag():
        for r0 in range(0, tile, sub):
            for h in range(heads):
                update(h, r0, r0 + sub, True)
        for h in range(heads):
            o_ref[:, head_cols[h]] = (acc_scr[h, :, 0:HEAD_DIM] / acc_scr[h, :, HEAD_DIM:]).astype(o_ref.dtype)


def _attend_prompt(q, k, k_bf, v_bf, *, out_dtype):
    t = q.shape[0]
    assert t % MOBA_BLOCK == 0
    nblk = t // MOBA_BLOCK
    assert nblk <= LANES
    n_sel = min(MOBA_TOPK, nblk)
    kmean = _block_means(k, nblk_pad=LANES)
    tile = _tile(t, 1024)
    assert tile % MOBA_BLOCK == 0
    pairs = [(i, j) for i in range(t // tile) for j in range(i + 1)]
    qi_arr = jnp.asarray([a for a, _ in pairs], jnp.int32)
    kj_arr = jnp.asarray([b for _, b in pairs], jnp.int32)
    k_onehot = (jnp.arange(t)[:, None] // MOBA_BLOCK == jnp.arange(LANES)[None, :]).astype(BF16)
    heads = 4
    width = heads * HEAD_DIM
    grid_spec = pltpu.PrefetchScalarGridSpec(
        num_scalar_prefetch=2,
        grid=(N_HEADS // heads, len(pairs)),
        in_specs=[
            pl.BlockSpec((tile, width), lambda h, p, qi, kj: (qi[p], h)),
            pl.BlockSpec((tile, width), lambda h, p, qi, kj: (kj[p], h)),
            pl.BlockSpec((tile, LANES), lambda h, p, qi, kj: (kj[p], 0)),
            pl.BlockSpec((tile, width), lambda h, p, qi, kj: (kj[p], h)),
            pl.BlockSpec((LANES, width), lambda h, p, qi, kj: (0, h)),
        ],
        out_specs=pl.BlockSpec((tile, width), lambda h, p, qi, kj: (qi[p], h)),
        scratch_shapes=[pltpu.VMEM((heads, tile, 2 * HEAD_DIM), BF16),
                        pltpu.VMEM((heads, tile, 1), F32),
                        pltpu.VMEM((heads, tile, 2 * HEAD_DIM), F32)],
    )
    return pl.pallas_call(
        functools.partial(_attn_kernel, tile=tile, sub=MOBA_BLOCK, heads=heads, n_sel=n_sel,
                          scale=HEAD_DIM ** -0.5),
        grid_spec=grid_spec,
        out_shape=jax.ShapeDtypeStruct((t, ATTN_WIDTH), out_dtype),
        compiler_params=_cparams("parallel", "arbitrary"),
        name="moba_prompt_attention",
    )(qi_arr, kj_arr, q, k_bf, k_onehot, v_bf, kmean)


def _page_head(page_ref, h):
    return page_ref[0, pl.ds(h, PAGE_SIZE, stride=N_HEADS), :]


def _paged_attn_kernel(pt_ref, *refs, gp, n_steps, dec, past, n_sel, scale):
    k_refs = refs[0:gp]
    v_refs = refs[gp:2 * gp]
    q_ref, kn_ref, vn_ref, o_ref = refs[2 * gp:2 * gp + 4]
    qbd_scr, kbf_scr, s_scr, ksum_scr, sel_scr, l_scr, oacc_scr, new_scr = refs[2 * gp + 4:]
    ph = pl.program_id(1)
    st = pl.program_id(2)
    page = PAGE_SIZE
    step_keys = gp * page
    t_past = past
    cols = N_HEADS * dec
    own = past // MOBA_BLOCK

    @pl.when((ph == 0) & (st == 0))
    def _init():
        new_scr[...] = jnp.zeros(new_scr.shape, F32)
        new_scr[0:dec, :] = q_ref[...]
        for h in range(N_HEADS):
            qt = new_scr[:, h * HEAD_DIM:(h + 1) * HEAD_DIM].T
            if h:
                qt = pltpu.roll(qt, h * dec, 1)
            qbd_scr[h * HEAD_DIM:(h + 1) * HEAD_DIM, :] = qt.astype(BF16)
        ksum_scr[...] = jnp.zeros(ksum_scr.shape, F32)

    @pl.when(ph == 0)
    def _scores():
        pages_per_blk = MOBA_BLOCK // page
        blks_per_step = gp // pages_per_blk
        sums = []
        for g in range(0, gp, pages_per_blk):
            for h in range(N_HEADS):
                tot = None
                for gg in range(g, g + pages_per_blk):
                    kh = _page_head(k_refs[gg], h)
                    kbf_scr[gg * page:(gg + 1) * page, h * HEAD_DIM:(h + 1) * HEAD_DIM] = kh.astype(BF16)
                    part = jnp.sum(kh, axis=0, keepdims=True)
                    tot = part if tot is None else tot + part
                sums.append(tot)
            s_blk = jnp.dot(kbf_scr[g * page:(g + pages_per_blk) * page, :], qbd_scr[...],
                            preferred_element_type=F32) * scale
            s_scr[pl.ds(pl.multiple_of(st * step_keys + g * page, MOBA_BLOCK), MOBA_BLOCK), :] = s_blk
        for s_static in range(n_steps):
            @pl.when(st == s_static)
            def _store_sums(s_static=s_static):
                for j, tot in enumerate(sums):
                    blk = s_static * blks_per_step + j // N_HEADS
                    h = j % N_HEADS
                    ksum_scr[blk:blk + 1, h * HEAD_DIM:(h + 1) * HEAD_DIM] = tot

    @pl.when((ph == 0) & (st == n_steps - 1))
    def _softmax():
        new_scr[...] = jnp.zeros(new_scr.shape, F32)
        new_scr[0:dec, :] = kn_ref[...]
        s_scr[t_past:t_past + page, :] = jnp.dot(new_scr[...].astype(BF16), qbd_scr[...],
                                                 preferred_element_type=F32) * scale
        own_col = jnp.full((dec, 1), own, jnp.int32)
        sel_scr[...] = jnp.zeros(sel_scr.shape, F32)
        for h in range(N_HEADS):
            kmean = ksum_scr[:, h * HEAD_DIM:(h + 1) * HEAD_DIM] / MOBA_BLOCK
            gate = _nt_dot(q_ref[:, h * HEAD_DIM:(h + 1) * HEAD_DIM], kmean, precision=HIGHEST)
            sel_scr[h * dec:(h + 1) * dec, :] = _select_blocks(gate, own_col, n_sel)
        sel_bf = sel_scr[...].astype(BF16)
        col = lax.broadcasted_iota(jnp.int32, (page, LANES), 1)
        krow = lax.broadcasted_iota(jnp.int32, (page, LANES), 0)
        new_ok = (krow <= col % dec) & (col < cols)
        s_new = jnp.where(new_ok, s_scr[t_past:t_past + page, :], NEG_BIG)
        s_scr[t_past:t_past + page, :] = s_new
        blk_col = lax.broadcasted_iota(jnp.int32, (step_keys, LANES), 1)
        key_row = lax.broadcasted_iota(jnp.int32, (step_keys, LANES), 0)

        def mask_body(ci, m):
            rows = pl.ds(pl.multiple_of(ci * step_keys, step_keys), step_keys)
            onehot = (lax.shift_right_logical(ci * step_keys + key_row, MOBA_BLOCK_SHIFT) == blk_col).astype(BF16)
            picked = _nt_dot(onehot, sel_bf)
            sc = jnp.where(picked > 0.5, s_scr[rows, :], NEG_BIG)
            s_scr[rows, :] = sc
            return jnp.maximum(m, jnp.max(sc, axis=0, keepdims=True))

        m = lax.fori_loop(0, n_steps, mask_body, jnp.max(s_new, axis=0, keepdims=True))

        def exp_body(ci, carry):
            rows = pl.ds(pl.multiple_of(ci * step_keys, step_keys), step_keys)
            s_scr[rows, :] = jnp.exp(s_scr[rows, :] - m)
            return carry

        lax.fori_loop(0, n_steps, exp_body, 0)
        s_scr[t_past:t_past + page, :] = jnp.exp(s_new - m)
        l_scr[...] = jnp.zeros(l_scr.shape, F32)
        oacc_scr[...] = jnp.zeros(oacc_scr.shape, F32)

    @pl.when(ph == 1)
    def _values():
        pages_per_blk = MOBA_BLOCK // page
        l_tot = None
        o_tot = None
        for g in range(0, gp, pages_per_blk):
            for gg in range(g, g + pages_per_blk):
                for h in range(N_HEADS):
                    kbf_scr[gg * page:(gg + 1) * page, h * HEAD_DIM:(h + 1) * HEAD_DIM] = _page_head(
                        v_refs[gg], h).astype(BF16)
            rows = pl.ds(pl.multiple_of(st * step_keys + g * page, MOBA_BLOCK), MOBA_BLOCK)
            pr = s_scr[rows, :].T
            l_blk = jnp.sum(pr, axis=1, keepdims=True)
            o_blk = jnp.dot(pr.astype(BF16), kbf_scr[g * page:(g + pages_per_blk) * page, :],
                            preferred_element_type=F32)
            l_tot = l_blk if l_tot is None else l_tot + l_blk
            o_tot = o_blk if o_tot is None else o_tot + o_blk
        l_scr[...] += l_tot
        oacc_scr[...] += o_tot

    @pl.when((ph == 1) & (st == n_steps - 1))
    def _fin():
        new_scr[...] = jnp.zeros(new_scr.shape, F32)
        new_scr[0:dec, :] = vn_ref[...]
        pr = s_scr[t_past:t_past + page, :].T
        l_tot = l_scr[...] + jnp.sum(pr, axis=1, keepdims=True)
        o_full = oacc_scr[...] + jnp.dot(pr.astype(BF16), new_scr[...].astype(BF16), preferred_element_type=F32)
        for h in range(N_HEADS):
            o_ref[:, h * HEAD_DIM:(h + 1) * HEAD_DIM] = (
                o_full[h * dec:(h + 1) * dec, h * HEAD_DIM:(h + 1) * HEAD_DIM] / l_tot[h * dec:(h + 1) * dec, :]
            ).astype(o_ref.dtype)


def _attend_paged(cache_k, cache_v, layer, page_table, q, k_new, v_new, *, dec):
    bsz, n_pages = page_table.shape
    past = n_pages * PAGE_SIZE
    assert past % MOBA_BLOCK == 0 and dec == SUBLANES and N_HEADS * dec <= LANES
    own = past // MOBA_BLOCK
    assert own + 1 <= LANES
    n_sel = min(MOBA_TOPK, own + 1)
    depth, n_pool = cache_k.shape[0], cache_k.shape[1]
    ck = cache_k.reshape(depth * n_pool, PAGE_SIZE * N_HEADS, HEAD_DIM)
    cv = cache_v.reshape(depth * n_pool, PAGE_SIZE * N_HEADS, HEAD_DIM)
    first = layer * n_pool
    gp = _tile(n_pages, 16)
    assert gp % (MOBA_BLOCK // PAGE_SIZE) == 0
    n_steps = n_pages // gp
    t_all = past + PAGE_SIZE

    def k_map(g):
        return lambda b, ph, st, pt: (first + pt[b, jnp.where(ph == 0, st, n_steps - 1) * gp + g], 0, 0)

    def v_map(g):
        return lambda b, ph, st, pt: (first + pt[b, jnp.where(ph == 1, st, 0) * gp + g], 0, 0)

    page_spec = lambda fn: pl.BlockSpec((1, PAGE_SIZE * N_HEADS, HEAD_DIM), fn)
    row_spec = pl.BlockSpec((dec, ATTN_WIDTH), lambda b, ph, st, pt: (b, 0))
    grid_spec = pltpu.PrefetchScalarGridSpec(
        num_scalar_prefetch=1,
        grid=(bsz, 2, n_steps),
        in_specs=[page_spec(k_map(g)) for g in range(gp)] + [page_spec(v_map(g)) for g in range(gp)]
        + [row_spec] * 3,
        out_specs=row_spec,
        scratch_shapes=[
            pltpu.VMEM((ATTN_WIDTH, LANES), BF16),
            pltpu.VMEM((gp * PAGE_SIZE, ATTN_WIDTH), BF16),
            pltpu.VMEM((t_all, LANES), F32),
            pltpu.VMEM((LANES, ATTN_WIDTH), F32),
            pltpu.VMEM((LANES, LANES), F32),
            pltpu.VMEM((LANES, 1), F32),
            pltpu.VMEM((LANES, ATTN_WIDTH), F32),
            pltpu.VMEM((PAGE_SIZE, ATTN_WIDTH), F32),
        ],
    )
    return pl.pallas_call(
        functools.partial(_paged_attn_kernel, gp=gp, n_steps=n_steps, dec=dec, past=past, n_sel=n_sel,
                          scale=HEAD_DIM ** -0.5),
        grid_spec=grid_spec,
        out_shape=jax.ShapeDtypeStruct((bsz * dec, ATTN_WIDTH), F32),
        compiler_params=_cparams("parallel", "arbitrary", "arbitrary"),
        name="moba_paged_attention",
    )(page_table, *([ck] * gp), *([cv] * gp), q, k_new, v_new)


def _layer_norm(x, g, b):
    mu = jnp.mean(x, axis=-1, keepdims=True)
    xc = x - mu
    var = jnp.mean(jnp.square(xc), axis=-1, keepdims=True)
    return xc * lax.rsqrt(var + LN_EPS) * g + b


def _merge_ln_kernel(oa_ref, ys_ref, ga_ref, gs_ref, res_ref, wa_ref, ws_ref, wo_ref, g_ref, b_ref, o_ref, obf_ref,
                     *, alpha, sub):
    for r0 in range(0, oa_ref.shape[0], sub):
        rows = slice(r0, r0 + sub)
        pa = jnp.dot(oa_ref[rows, :].astype(BF16), wa_ref[...], preferred_element_type=F32)
        ps = jnp.dot(ys_ref[rows, :].astype(BF16), ws_ref[...], preferred_element_type=F32)
        merged = jax.nn.sigmoid(ga_ref[rows, :]) * pa + jax.nn.sigmoid(gs_ref[rows, :]) * ps
        acc = jnp.dot(merged.astype(BF16), wo_ref[...], preferred_element_type=F32)
        y = _layer_norm(alpha * res_ref[rows, :] + acc, g_ref[...], b_ref[...])
        o_ref[rows, :] = y
        obf_ref[rows, :] = y.astype(BF16)


def _merge_ln(o_attn, y_s, rest, res, w_pa, w_ps, w_out, g, b, *, ga_col, gs_col, alpha, tm=256):
    m = o_attn.shape[0]
    tm = _tile(m, tm)
    row = lambda width, col=0: pl.BlockSpec((tm, width), lambda i: (i, col))
    resident = lambda shape: pl.BlockSpec(shape, lambda i: (0, 0), pipeline_mode=pl.Buffered(1))
    return pl.pallas_call(
        functools.partial(_merge_ln_kernel, alpha=alpha, sub=_tile(tm, 128)),
        grid=(m // tm,),
        in_specs=[row(ATTN_WIDTH), row(SSM_INNER), row(D_MODEL, ga_col), row(D_MODEL, gs_col), row(D_MODEL),
                  resident((ATTN_WIDTH, D_MODEL)), resident((SSM_INNER, D_MODEL)), resident((D_MODEL, D_MODEL)),
                  resident((1, D_MODEL)), resident((1, D_MODEL))],
        out_specs=[row(D_MODEL), row(D_MODEL)],
        out_shape=[jax.ShapeDtypeStruct((m, D_MODEL), F32), jax.ShapeDtypeStruct((m, D_MODEL), BF16)],
        compiler_params=_cparams("parallel"),
        name="merge_out_proj_layernorm",
    )(o_attn, y_s, rest, rest, res, w_pa, w_ps, w_out, g, b)


def _gelu(x):
    return 0.5 * x * (1.0 + lax.erf(x * np.float32(np.sqrt(0.5))))


def _glu_long_kernel(h_ref, wg_ref, wu_ref, st_ref, cw_ref, cb_ref, act_ref, tail_ref, ext_scr, *, tm, n_row_tiles):
    i = pl.program_id(1)
    halo = SUBLANES

    @pl.when(i == 0)
    def _init():
        ext_scr[0:halo, :] = st_ref[...]

    sub = _tile(tm, 256)
    for r0 in range(0, tm, sub):
        hh = h_ref[r0:r0 + sub, :]
        g_up = jnp.dot(hh, wg_ref[...], preferred_element_type=F32)
        u = jnp.dot(hh, wu_ref[...], preferred_element_type=F32)
        ext_scr[halo + r0:halo + r0 + sub, :] = g_up
        first = halo - (FFN_CONV - 1) + r0
        acc = ext_scr[first:first + sub, :] * cw_ref[0:1, :]
        for kk in range(1, FFN_CONV):
            acc = acc + ext_scr[first + kk:first + kk + sub, :] * cw_ref[kk:kk + 1, :]
        act_ref[r0:r0 + sub, :] = (_gelu(acc + cb_ref[...]) * u).astype(act_ref.dtype)
    ext_scr[0:halo, :] = ext_scr[tm:tm + halo, :]

    @pl.when(i == n_row_tiles - 1)
    def _tail():
        tail_ref[...] = ext_scr[tm:tm + halo, :]


def _glu_long(h_bf, w_gate, w_up, state, conv_w, conv_b, *, tm=1024, tn=512):
    m, k = h_bf.shape
    tm = _tile(m, tm)
    n_row_tiles = m // tm
    return pl.pallas_call(
        functools.partial(_glu_long_kernel, tm=tm, n_row_tiles=n_row_tiles),
        grid=(D_FF // tn, n_row_tiles),
        in_specs=[pl.BlockSpec((tm, k), lambda j, i: (i, 0)), pl.BlockSpec((k, tn), lambda j, i: (0, j)),
                  pl.BlockSpec((k, tn), lambda j, i: (0, j)), pl.BlockSpec((SUBLANES, tn), lambda j, i: (0, j)),
                  pl.BlockSpec((FFN_CONV, tn), lambda j, i: (0, j)), pl.BlockSpec((1, tn), lambda j, i: (0, j))],
        out_specs=[pl.BlockSpec((tm, tn), lambda j, i: (i, j)), pl.BlockSpec((SUBLANES, tn), lambda j, i: (0, j))],
        out_shape=[jax.ShapeDtypeStruct((m, D_FF), BF16), jax.ShapeDtypeStruct((SUBLANES, D_FF), F32)],
        scratch_shapes=[pltpu.VMEM((SUBLANES + tm, tn), F32)],
        compiler_params=_cparams("parallel", "arbitrary"),
        name="convglu_up_long",
    )(h_bf, w_gate, w_up, state, conv_w, conv_b)


def _glu_short_kernel(h_ref, wg_ref, wu_ref, p1_ref, p2_ref, cw_ref, cb_ref, act_ref, gup_ref, *, seq):
    hh = h_ref[...]
    g_up = jnp.dot(hh, wg_ref[...], preferred_element_type=F32)
    u = jnp.dot(hh, wu_ref[...], preferred_element_type=F32)
    gup_ref[...] = g_up
    r = lax.broadcasted_iota(jnp.int32, g_up.shape, 0) % seq
    back1 = jnp.where(r >= 1, pltpu.roll(g_up, 1, 0), p1_ref[...])
    back2 = jnp.where(r >= 2, pltpu.roll(g_up, 2, 0), p2_ref[...])
    acc = back2 * cw_ref[0:1, :] + back1 * cw_ref[1:2, :] + g_up * cw_ref[2:3, :]
    act_ref[...] = (_gelu(acc + cb_ref[...]) * u).astype(act_ref.dtype)


def _glu_short(h_bf, w_gate, w_up, p1, p2, conv_w, conv_b, *, seq, tn=512):
    m, k = h_bf.shape
    full = lambda j: (0, j)
    return pl.pallas_call(
        functools.partial(_glu_short_kernel, seq=seq),
        grid=(D_FF // tn,),
        in_specs=[pl.BlockSpec((m, k), lambda j: (0, 0)), pl.BlockSpec((k, tn), full), pl.BlockSpec((k, tn), full),
                  pl.BlockSpec((m, tn), full), pl.BlockSpec((m, tn), full),
                  pl.BlockSpec((FFN_CONV, tn), full), pl.BlockSpec((1, tn), full)],
        out_specs=[pl.BlockSpec((m, tn), full), pl.BlockSpec((m, tn), full)],
        out_shape=[jax.ShapeDtypeStruct((m, D_FF), BF16), jax.ShapeDtypeStruct((m, D_FF), F32)],
        compiler_params=_cparams("parallel"),
        name="convglu_up_short",
    )(h_bf, w_gate, w_up, p1, p2, conv_w, conv_b)


def _down_ln_kernel(a_ref, w_ref, res_ref, g_ref, b_ref, o_ref, *, alpha, sub):
    for r0 in range(0, a_ref.shape[0], sub):
        rows = slice(r0, r0 + sub)
        acc = jnp.dot(a_ref[rows, :], w_ref[...], preferred_element_type=F32)
        o_ref[rows, :] = _layer_norm(alpha * res_ref[rows, :] + acc, g_ref[...], b_ref[...])


def _down_ln(a, w, res, g, b, *, alpha, tm=256):
    m, k = a.shape
    n = w.shape[1]
    tm = _tile(m, tm)
    resident = lambda shape: pl.BlockSpec(shape, lambda i: (0, 0), pipeline_mode=pl.Buffered(1))
    return pl.pallas_call(
        functools.partial(_down_ln_kernel, alpha=alpha, sub=_tile(tm, 128)),
        grid=(m // tm,),
        in_specs=[pl.BlockSpec((tm, k), lambda i: (i, 0)), resident((k, n)), pl.BlockSpec((tm, n), lambda i: (i, 0)),
                  resident((1, n)), resident((1, n))],
        out_specs=pl.BlockSpec((tm, n), lambda i: (i, 0)),
        out_shape=jax.ShapeDtypeStruct((m, n), F32),
        compiler_params=_cparams("parallel"),
        name="down_proj_layernorm",
    )(a, w, res, g, b)


REST_Z, REST_GA, REST_GS = 0, 1, 2
REST_XBC = 3 * D_MODEL // CONV_DIM
assert REST_XBC * CONV_DIM == 3 * D_MODEL


def _prep_layer(w_in, ssm_conv_w, ssm_conv_b, dt_bias, a_log, d_skip, ssm_norm_g, w_proj_attn, w_proj_ssm, w_out,
                ln1_g, ln1_b, w_gate, w_up, ffn_conv_w, ffn_conv_b, w_down, ln2_g, ln2_b):
    widths = (ATTN_WIDTH, ATTN_WIDTH, ATTN_WIDTH, SSM_INNER, CONV_DIM, SSM_HEADS, D_MODEL, D_MODEL)
    offs = np.concatenate([[0], np.cumsum(widths)])
    wq, wk, wv, wz, wxbc, wdt, wga, wgs = (w_in[:, offs[i]:offs[i + 1]] for i in range(8))
    row = lambda v: v.reshape(1, -1).astype(F32)
    pad_lanes = lambda v: jnp.pad(row(v), ((0, 0), (0, LANES - v.shape[-1])))
    return {
        "w_q": wq.astype(BF16), "w_k": wk.astype(BF16), "w_v": wv.astype(BF16),
        "w_rest": jnp.concatenate([wz, wga, wgs, wxbc], axis=1).astype(BF16),
        "w_dt": jnp.pad(wdt, ((0, 0), (0, LANES - SSM_HEADS))).astype(BF16),
        "conv_w": ssm_conv_w.astype(F32), "conv_b": row(ssm_conv_b),
        "dt_bias": pad_lanes(dt_bias), "a_log": pad_lanes(a_log),
        "d_full": row(jnp.repeat(d_skip, SSM_HEAD_DIM)), "norm_g": row(ssm_norm_g),
        "expand": (jnp.arange(LANES)[:, None] == jnp.arange(SSM_INNER)[None, :] // SSM_HEAD_DIM).astype(BF16),
        "w_pa": w_proj_attn.astype(BF16), "w_ps": w_proj_ssm.astype(BF16), "w_out": w_out.astype(BF16),
        "ln1_g": row(ln1_g), "ln1_b": row(ln1_b),
        "w_gate": w_gate.astype(BF16), "w_up": w_up.astype(BF16),
        "ffn_conv_w": ffn_conv_w.astype(F32), "ffn_conv_b": row(ffn_conv_b),
        "w_down": w_down.astype(BF16), "ln2_g": row(ln2_g), "ln2_b": row(ln2_b),
    }


def _rope_tables(pos):
    half = HEAD_DIM // 2
    inv = ROPE_THETA ** (-jnp.arange(half, dtype=F32) * 2.0 / HEAD_DIM)
    ang = pos.astype(F32)[:, None] * inv[None, :]
    cos, sin = jnp.cos(ang), jnp.sin(ang)
    return jnp.concatenate([cos, cos], axis=1), jnp.concatenate([-sin, sin], axis=1)


def _layer(x, pos, p, *, conv_state, ssm_state, ffn_state, paged, alpha):
    bsz, seq, _ = x.shape
    m = bsz * seq
    x2 = x.reshape(m, D_MODEL)
    x_bf = x2.astype(BF16)
    cos, sin = _rope_tables(pos)
    if bsz > 1:
        cos, sin = jnp.tile(cos, (bsz, 1)), jnp.tile(sin, (bsz, 1))
    prompt = paged is None
    q = _project(x_bf, p["w_q"], (cos, sin), tn=ATTN_WIDTH, name="in_proj_q_rope")
    k_new = _project(x_bf, p["w_k"], (cos, sin), with_bf16=prompt, tn=ATTN_WIDTH, name="in_proj_k_rope")
    v_new = _project(x_bf, p["w_v"], with_bf16=prompt, tn=ATTN_WIDTH, name="in_proj_v")
    if prompt:
        (k_new, k_bf), (v_new, v_bf) = k_new, v_new
    rest = _project(x_bf, p["w_rest"], tn=1024, name="in_proj_rest")
    dt_raw = _project(x_bf, p["w_dt"], tn=LANES, name="in_proj_dt")

    assert seq >= SSM_CONV - 1
    new_conv = rest.reshape(bsz, seq, -1)[:, seq - (SSM_CONV - 1):, REST_XBC * CONV_DIM:(REST_XBC + 1) * CONV_DIM]

    if prompt:
        assert bsz == 1
        o_attn = _attend_prompt(q, k_new, k_bf, v_bf, out_dtype=BF16)
    else:
        cache_k, cache_v, layer, page_table = paged
        o_attn = _attend_paged(cache_k, cache_v, layer, page_table, q, k_new, v_new, dec=seq)

    cst = jnp.pad(conv_state.astype(F32), ((0, 0), (SUBLANES - (SSM_CONV - 1), 0), (0, 0)))
    h0 = ssm_state.astype(F32).reshape(bsz, SSM_GROUPS, SSM_GROUP_WIDTH, SSM_STATE)
    y_s, h_last = _ssd(rest.reshape(bsz, seq, -1), dt_raw.reshape(bsz, seq, LANES), cst, h0, p,
                       xbc_col=REST_XBC, z_col=REST_Z, out_dtype=BF16 if seq % 16 == 0 else F32)
    h_new = h_last.reshape(bsz, SSM_HEADS, SSM_HEAD_DIM, SSM_STATE)

    h1, h1_bf = _merge_ln(o_attn, y_s.reshape(m, SSM_INNER), rest, x2, p["w_pa"], p["w_ps"], p["w_out"],
                          p["ln1_g"], p["ln1_b"], ga_col=REST_GA, gs_col=REST_GS, alpha=alpha)

    if bsz == 1:
        st = jnp.pad(ffn_state[0].astype(F32), ((SUBLANES - (FFN_CONV - 1), 0), (0, 0)))
        act, tail = _glu_long(h1_bf, p["w_gate"], p["w_up"], st, p["ffn_conv_w"], p["ffn_conv_b"])
        assert seq >= FFN_CONV - 1
        new_ffn = tail[None, SUBLANES - (FFN_CONV - 1):]
    else:
        assert seq == SUBLANES
        fs = ffn_state.astype(F32)
        p1 = jnp.pad(fs[:, 1:2], ((0, 0), (0, seq - 1), (0, 0))).reshape(m, D_FF)
        p2 = jnp.pad(fs, ((0, 0), (0, seq - 2), (0, 0))).reshape(m, D_FF)
        act, g_up = _glu_short(h1_bf, p["w_gate"], p["w_up"], p1, p2, p["ffn_conv_w"], p["ffn_conv_b"], seq=seq)
        new_ffn = g_up.reshape(bsz, seq, D_FF)[:, seq - (FFN_CONV - 1):]
    y = _down_ln(act, p["w_down"], h1, p["ln2_g"], p["ln2_b"], alpha=alpha)
    return (y.reshape(bsz, seq, D_MODEL), k_new.reshape(bsz, seq, N_HEADS, HEAD_DIM),
            v_new.reshape(bsz, seq, N_HEADS, HEAD_DIM), h_new, new_conv, new_ffn)


def kernel(x_prompt, x_sample, cache_k, cache_v, state_ssm, state_conv, state_ffn_conv, page_table, w_in, ssm_conv_w, ssm_conv_b, dt_bias, a_log, d_skip, ssm_norm_g, w_proj_attn, w_proj_ssm, w_out, ln1_g, ln1_b, w_gate, w_up, ffn_conv_w, ffn_conv_b, w_down, ln2_g, ln2_b):
    depth = w_in.shape[0]
    alpha = (2.0 * depth) ** 0.25
    bp, lp = x_prompt.shape[0], x_prompt.shape[1]
    past_len = page_table.shape[1] * PAGE_SIZE
    pos_p = jnp.arange(lp, dtype=jnp.int32)
    pos_s = past_len + jnp.arange(x_sample.shape[1], dtype=jnp.int32)
    hp, hs = x_prompt, x_sample
    outs_p, outs_s = [], []
    for l in range(depth):
        p = _prep_layer(w_in[l], ssm_conv_w[l], ssm_conv_b[l], dt_bias[l], a_log[l], d_skip[l], ssm_norm_g[l],
                        w_proj_attn[l], w_proj_ssm[l], w_out[l], ln1_g[l], ln1_b[l], w_gate[l], w_up[l],
                        ffn_conv_w[l], ffn_conv_b[l], w_down[l], ln2_g[l], ln2_b[l])
        conv0 = jnp.zeros((bp, SSM_CONV - 1, CONV_DIM), F32)
        ssm0 = jnp.zeros((bp, SSM_HEADS, SSM_HEAD_DIM, SSM_STATE), F32)
        ffn0 = jnp.zeros((bp, FFN_CONV - 1, D_FF), F32)
        hp, *rest_p = _layer(hp, pos_p, p, conv_state=conv0, ssm_state=ssm0, ffn_state=ffn0, paged=None, alpha=alpha)
        hs, *rest_s = _layer(hs, pos_s, p, conv_state=state_conv[l], ssm_state=state_ssm[l],
                             ffn_state=state_ffn_conv[l], paged=(cache_k, cache_v, l, page_table), alpha=alpha)
        outs_p.append(rest_p)
        outs_s.append(rest_s)
    stack = lambda outs, i: jnp.stack([o[i] for o in outs])
    return (hp, hs, *(stack(outs_p, i) for i in range(5)), *(stack(outs_s, i) for i in range(5)))
```
